```python
import math
import jax
import jax.numpy as jnp
from jax import lax
import numpy as np

D_MODEL = 1024
BATCH = 4
SEQ = 8192
DEPTH = 2

GRID_W = 64
CTX_LEN = 256
EPS = 1e-6

DN_HEADS = 8
DN_HEAD_DIM = 64
DN_DIM = DN_HEADS * DN_HEAD_DIM
DN_CONV_W = 5
DN_CHUNK = 64

AT_Q_HEADS = 8
AT_KV_HEADS = 2
AT_GROUP = AT_Q_HEADS // AT_KV_HEADS
AT_HEAD_DIM = 64
AT_Q_DIM = AT_Q_HEADS * AT_HEAD_DIM
AT_KV_DIM = AT_KV_HEADS * AT_HEAD_DIM
Q_BLOCK = 128
ROPE_THETA = 10000.0
ROPE_AXIS_DIM = AT_HEAD_DIM // 2

FFN_DIM = 2816
N_EXPERTS = 8
TOP_K = 2
EXPERT_DIM = 3584
MOE_BLOCK = 256
N_DENSE = (DEPTH + 1) // 2
N_MOE = DEPTH // 2

OFF_DN_Z = 3 * DN_DIM
OFF_DN_A = OFF_DN_Z + DN_DIM
OFF_DN_B = OFF_DN_A + 2 * DN_HEADS
OFF_AT_Q = OFF_DN_B + 2 * DN_HEADS
OFF_AT_K = OFF_AT_Q + AT_Q_DIM
OFF_AT_V = OFF_AT_K + AT_KV_DIM
IN_DIM = OFF_AT_V + AT_KV_DIM
MIX_DIM = DN_DIM + AT_Q_DIM

kernel_name = 'hybrid_deltanet_axial_gqa_moe_prefix_dit'


def _rmsnorm(x, w):
    x32 = x.astype(jnp.float32)
    y = x32 * lax.rsqrt(jnp.mean(x32 * x32, axis=-1, keepdims=True) + EPS)
    return (y * w.astype(jnp.float32)).astype(x.dtype)


def _modulate(h, shift, scale):
    return h * (1.0 + scale) + shift


def _l2norm(x):
    return x * lax.rsqrt(jnp.sum(x * x, axis=-1, keepdims=True) + EPS)


def _centred_dwconv(x, w):
    pad = (w.shape[0] - 1) // 2
    return lax.conv_general_dilated(
        x, w[:, None, :], window_strides=(1,), padding=[(pad, pad)],
        dimension_numbers=('NWC', 'WIO', 'NWC'), feature_group_count=x.shape[-1])


def _axial_rope_tables(t):
    rows = t // GRID_W
    row_pos = jnp.repeat(jnp.arange(rows, dtype=jnp.float32), GRID_W, total_repeat_length=t)
    col_pos = jnp.tile(jnp.arange(GRID_W, dtype=jnp.float32), rows)
    n_freq = ROPE_AXIS_DIM // 2
    freqs = ROPE_THETA ** (-jnp.arange(n_freq, dtype=jnp.float32) / n_freq)
    ang = jnp.concatenate([row_pos[:, None] * freqs, col_pos[:, None] * freqs], axis=-1)
    return jnp.cos(ang), jnp.sin(ang)


def _apply_rope(x, cos, sin):
    half = x.shape[-1] // 2
    x32 = x.astype(jnp.float32)
    x1, x2 = x32[..., :half], x32[..., half:]
    cos, sin = cos[None, :, None, :], sin[None, :, None, :]
    return jnp.concatenate([x1 * cos - x2 * sin, x2 * cos + x1 * sin], axis=-1).astype(x.dtype)


def _gated_delta_chunked(q, k, v, g, beta, s0):
    b, h, t, dk = q.shape
    dv = v.shape[-1]
    n = t // DN_CHUNK
    chunk = lambda z: z.reshape((b, h, n, DN_CHUNK) + z.shape[3:])
    q, k, v, g, beta = chunk(q), chunk(k), chunk(v), chunk(g), chunk(beta)
    g = jnp.cumsum(g, axis=-1)
    incl = jnp.tril(jnp.ones((DN_CHUNK, DN_CHUNK), dtype=bool))
    strict = jnp.tril(jnp.ones((DN_CHUNK, DN_CHUNK), dtype=bool), -1)
    diff = g[..., :, None] - g[..., None, :]
    decay = jnp.where(incl, jnp.exp(jnp.where(incl, diff, 0.0)), 0.0)
    k_beta = k * beta[..., None]
    a_low = jnp.where(strict, jnp.einsum('bhncd,bhnsd->bhncs', k_beta, k) * decay, 0.0)
    rhs = jnp.concatenate([v * beta[..., None], k_beta * jnp.exp(g)[..., None]], axis=-1)
    eye = jnp.eye(DN_CHUNK, dtype=q.dtype)
    sol = lax.linalg.triangular_solve(a_low + eye, rhs, left_side=True, lower=True, unit_diagonal=True)
    u, w = sol[..., :dv], sol[..., dv:]
    attn = jnp.where(incl, jnp.einsum('bhncd,bhnsd->bhncs', q, k) * decay, 0.0)
    q_dec = q * jnp.exp(g)[..., None]
    k_dec = k * jnp.exp(g[..., -1:] - g)[..., None]
    g_last = jnp.exp(g[..., -1])

    def step(s, xs):
        u_c, w_c, attn_c, q_c, k_c, gl_c = xs
        v_new = u_c - jnp.einsum('bhcd,bhde->bhce', w_c, s)
        o_c = jnp.einsum('bhcd,bhde->bhce', q_c, s) + jnp.einsum('bhcs,bhse->bhce', attn_c, v_new)
        s = s * gl_c[..., None, None] + jnp.einsum('bhcd,bhce->bhde', k_c, v_new)
        return s, o_c

    xs = tuple(jnp.moveaxis(z, 2, 0) for z in (u, w, attn, q_dec, k_dec, g_last))
    s_final, o = lax.scan(step, s0, xs)
    o = jnp.moveaxis(o, 0, 2).reshape(b, h, t, dv)
    return o, s_final


def _dn_streams(p, conv_w, a_log, dt_bias):
    b, t, _ = p.shape
    p = p.astype(jnp.float32)
    qkv = jax.nn.silu(_centred_dwconv(p[..., :3 * DN_DIM], conv_w.astype(jnp.float32)))
    q, k, v = jnp.split(qkv, 3, axis=-1)
    heads = lambda z: z.reshape(b, t, DN_HEADS, DN_HEAD_DIM).transpose(0, 2, 1, 3)
    q = _l2norm(heads(q)) * (DN_HEAD_DIM ** -0.5)
    k = _l2norm(heads(k))
    v = heads(v)
    a = p[..., OFF_DN_A:OFF_DN_A + 2 * DN_HEADS].reshape(b, t, 2, DN_HEADS)
    bb = p[..., OFF_DN_B:OFF_DN_B + 2 * DN_HEADS].reshape(b, t, 2, DN_HEADS)
    g = -jnp.exp(a_log.astype(jnp.float32)) * jax.nn.softplus(a + dt_bias.astype(jnp.float32))
    beta = jax.nn.sigmoid(bb)
    return q, k, v, g.transpose(2, 0, 3, 1), beta.transpose(2, 0, 3, 1)


def _bidir_delta(q, k, v, g, beta, s0_f, s0_b):
    flip = lambda z: jnp.flip(z, axis=2)
    o_f, s_f = _gated_delta_chunked(q, k, v, g[0], beta[0], s0_f)
    o_b, s_b = _gated_delta_chunked(flip(q), flip(k), flip(v), flip(g[1]), flip(beta[1]), s0_b)
    return o_f + flip(o_b), s_f, s_b


def _dn_output(o, p, norm_w):
    b, h, t, dv = o.shape
    o = o.transpose(0, 2, 1, 3)
    o = o * lax.rsqrt(jnp.mean(o * o, axis=-1, keepdims=True) + EPS) * norm_w.astype(jnp.float32)
    z = p[..., OFF_DN_Z:OFF_DN_Z + DN_DIM].astype(jnp.float32).reshape(b, t, h, dv)
    return (o * jax.nn.silu(z)).reshape(b, t, DN_DIM).astype(p.dtype)


def _gated_deltanet(p_lat, p_ctx, conv_w, a_log, dt_bias, norm_w, need_ctx_out):
    b = p_lat.shape[0]
    zero = jnp.zeros((b, DN_HEADS, DN_HEAD_DIM, DN_HEAD_DIM), jnp.float32)
    qc, kc, vc, gc, bc = _dn_streams(p_ctx, conv_w, a_log, dt_bias)
    o_c, s_f, s_b = _bidir_delta(qc, kc, vc, gc, bc, zero, zero)
    q, k, v, g, be = _dn_streams(p_lat, conv_w, a_log, dt_bias)
    o, _, _ = _bidir_delta(q, k, v, g, be, s_f, s_b)
    lat = _dn_output(o, p_lat, norm_w)
    ctx = _dn_output(o_c, p_ctx, norm_w) if need_ctx_out else None
    return lat, ctx


def _attend(qb, kk, vv):
    s = jnp.einsum('bqkgd,bskd->bkgqs', qb, kk).astype(jnp.float32) * (AT_HEAD_DIM ** -0.5)
    pr = jax.nn.softmax(s, axis=-1).astype(vv.dtype)
    return jnp.einsum('bkgqs,bskd->bqkgd', pr, vv)


def _axial_gqa(p_lat, p_ctx, qn_w, kn_w, need_ctx_out):
    b, t, _ = p_lat.shape
    lc = p_ctx.shape[1]
    q_of = lambda p: _rmsnorm(p[..., OFF_AT_Q:OFF_AT_Q + AT_Q_DIM].reshape(p.shape[0], p.shape[1], AT_Q_HEADS, AT_HEAD_DIM), qn_w)
    k_of = lambda p: _rmsnorm(p[..., OFF_AT_K:OFF_AT_K + AT_KV_DIM].reshape(p.shape[0], p.shape[1], AT_KV_HEADS, AT_HEAD_DIM), kn_w)
    v_of = lambda p: p[..., OFF_AT_V:OFF_AT_V + AT_KV_DIM].reshape(p.shape[0], p.shape[1], AT_KV_HEADS, AT_HEAD_DIM)
    cos, sin = _axial_rope_tables(t)
    q = _apply_rope(q_of(p_lat), cos, sin)
    k = _apply_rope(k_of(p_lat), cos, sin)
    kc, vc = k_of(p_ctx), v_of(p_ctx)
    k_all = jnp.concatenate([k, kc], axis=1)
    v_all = jnp.concatenate([v_of(p_lat), vc], axis=1)
    nb = t // Q_BLOCK
    qb = q.reshape(b, nb, Q_BLOCK, AT_KV_HEADS, AT_GROUP, AT_HEAD_DIM).transpose(1, 0, 2, 3, 4, 5)
    o = lax.map(lambda blk: _attend(blk, k_all, v_all), qb)
    lat = o.transpose(1, 0, 2, 3, 4, 5).reshape(b, t, AT_Q_DIM)
    ctx = None
    if need_ctx_out:
        qc = q_of(p_ctx).reshape(b, lc, AT_KV_HEADS, AT_GROUP, AT_HEAD_DIM)
        ctx = _attend(qc, kc, vc).reshape(b, lc, AT_Q_DIM)
    return lat, ctx


def _swiglu(h, w_gate, w_up, w_down):
    return (jax.nn.silu(h @ w_gate) * (h @ w_up)) @ w_down


def _moe_swiglu(h, router_w, router_b, w_gate, w_up, w_down):
    b, l, d = h.shape
    hf = h.reshape(-1, d)
    n = hf.shape[0]
    logits = (hf @ router_w).astype(jnp.float32) + router_b.astype(jnp.float32)
    top_val, top_e = lax.top_k(logits, TOP_K)
    top_w = jax.nn.softmax(top_val, axis=-1)
    m = n * TOP_K
    flat_e = top_e.reshape(m)
    flat_tok = jnp.repeat(jnp.arange(n, dtype=jnp.int32), TOP_K)
    flat_w = top_w.reshape(m)
    order = jnp.argsort(flat_e, stable=True)
    sorted_e = flat_e[order]
    counts = jnp.bincount(flat_e, length=N_EXPERTS)
    padded = (counts + MOE_BLOCK - 1) // MOE_BLOCK * MOE_BLOCK
    start = jnp.cumsum(counts) - counts
    pad_end = jnp.cumsum(padded)
    pad_start = pad_end - padded
    dest = pad_start[sorted_e] + jnp.arange(m, dtype=jnp.int32) - start[sorted_e]
    p_rows = ((m + MOE_BLOCK - 1) // MOE_BLOCK + N_EXPERTS) * MOE_BLOCK
    row_tok = jnp.full((p_rows,), n, jnp.int32).at[dest].set(flat_tok[order])
    row_w = jnp.zeros((p_rows,), jnp.float32).at[dest].set(flat_w[order])
    nblk = p_rows // MOE_BLOCK
    blk_e = jnp.minimum(jnp.searchsorted(pad_end, jnp.arange(nblk, dtype=pad_end.dtype) * MOE_BLOCK, side='right'), N_EXPERTS - 1)
    xs = jnp.concatenate([hf, jnp.zeros((1, d), hf.dtype)], axis=0)[row_tok].reshape(nblk, MOE_BLOCK, d)

    def expert_block(args):
        xb, e = args
        return _swiglu(xb, w_gate[e], w_up[e], w_down[e])

    ys = lax.map(expert_block, (xs, blk_e)).reshape(p_rows, d)
    out = jnp.zeros((n + 1, d), ys.dtype).at[row_tok].add(ys * row_w[:, None].astype(ys.dtype))
    return out[:n].reshape(b, l, d)


def setup_inputs(seed: int = 0) -> dict:
    key = jax.random.key(seed)
    ks = jax.random.split(key, 32)
    f32 = jnp.float32
    D, L = D_MODEL, DEPTH
    nrm = lambda k, shape, s: jax.random.normal(k, shape, f32) * s
    dt = jnp.exp(jax.random.uniform(ks[10], (L, 2, DN_HEADS), f32, math.log(1e-3), math.log(1e-1)))
    return {
        'x': nrm(ks[0], (BATCH, SEQ, D), 1.0),
        'c': nrm(ks[1], (BATCH, D), 1.0),
        'ctx': nrm(ks[2], (BATCH, CTX_LEN, D), 1.0),
        'c_ctx': nrm(ks[3], (D,), 1.0),
        'w_mod': nrm(ks[4], (L, D, 6 * D), 0.5 * D ** -0.5),
        'b_mod': nrm(ks[5], (L, 6 * D), 0.02),
        'norm1_w': 1.0 + nrm(ks[6], (L, D), 0.02),
        'norm2_w': 1.0 + nrm(ks[7], (L, D), 0.02),
        'w_in': nrm(ks[8], (L, D, IN_DIM), D ** -0.5),
        'dn_conv_w': nrm(ks[9], (L, DN_CONV_W, 3 * DN_DIM), DN_CONV_W ** -0.5),
        'dn_a_log': jnp.log(jax.random.uniform(ks[11], (L, 2, DN_HEADS), f32, 1.0, 16.0)),
        'dn_dt_bias': dt + jnp.log(-jnp.expm1(-dt)),
        'dn_norm_w': 1.0 + nrm(ks[12], (L, DN_HEAD_DIM), 0.02),
        'at_qnorm_w': 1.0 + nrm(ks[13], (L, AT_HEAD_DIM), 0.02),
        'at_knorm_w': 1.0 + nrm(ks[14], (L, AT_HEAD_DIM), 0.02),
        'w_out': nrm(ks[15], (L, MIX_DIM, D), MIX_DIM ** -0.5),
        'ffn_w_gate': nrm(ks[16], (N_DENSE, D, FFN_DIM), D ** -0.5),
        'ffn_w_up': nrm(ks[17], (N_DENSE, D, FFN_DIM), D ** -0.5),
        'ffn_w_down': nrm(ks[18], (N_DENSE, FFN_DIM, D), FFN_DIM ** -0.5),
        'moe_router_w': nrm(ks[19], (N_MOE, D, N_EXPERTS), D ** -0.5),
        'moe_router_b': nrm(ks[20], (N_MOE, N_EXPERTS), 0.01),
        'moe_w_gate': nrm(ks[21], (N_MOE, N_EXPERTS, D, EXPERT_DIM), D ** -0.5),
        'moe_w_up': nrm(ks[22], (N_MOE, N_EXPERTS, D, EXPERT_DIM), D ** -0.5),
        'moe_w_down': nrm(ks[23], (N_MOE, N_EXPERTS, EXPERT_DIM, D), EXPERT_DIM ** -0.5),
        'final_norm_w': 1.0 + nrm(ks[24], (D,), 0.02),
    }


def reference(x, c, ctx, c_ctx, w_mod, b_mod, norm1_w, norm2_w, w_in, dn_conv_w, dn_a_log,
              dn_dt_bias, dn_norm_w, at_qnorm_w, at_knorm_w, w_out, ffn_w_gate, ffn_w_up,
              ffn_w_down, moe_router_w, moe_router_b, moe_w_gate, moe_w_up, moe_w_down,
              final_norm_w):
    xc = ctx
    sc = jax.nn.silu(c)
    scc = jax.nn.silu(c_ctx)
    for layer in range(DEPTH):
        last = layer == DEPTH - 1
        mod = (sc @ w_mod[layer] + b_mod[layer])[:, None, :]
        mod_c = scc @ w_mod[layer] + b_mod[layer]
        sh1, sc1, g1, sh2, sc2, g2 = jnp.split(mod, 6, axis=-1)
        sh1c, sc1c, g1c, sh2c, sc2c, g2c = jnp.split(mod_c, 6, axis=-1)
        p = _modulate(_rmsnorm(x, norm1_w[layer]), sh1, sc1) @ w_in[layer]
        pc = _modulate(_rmsnorm(xc, norm1_w[layer]), sh1c, sc1c) @ w_in[layer]
        dn, dn_c = _gated_deltanet(p, pc, dn_conv_w[layer], dn_a_log[layer], dn_dt_bias[layer],
                                   dn_norm_w[layer], not last)
        at, at_c = _axial_gqa(p, pc, at_qnorm_w[layer], at_knorm_w[layer], not last)
        x = x + g1 * (jnp.concatenate([dn, at], axis=-1) @ w_out[layer])
        if layer % 2 == 0:
            i = layer // 2
            ffn = lambda t: _swiglu(t, ffn_w_gate[i], ffn_w_up[i], ffn_w_down[i])
        else:
            j = layer // 2
            ffn = lambda t: _moe_swiglu(t, moe_router_w[j], moe_router_b[j], moe_w_gate[j],
                                        moe_w_up[j], moe_w_down[j])
        x = x + g2 * ffn(_modulate(_rmsnorm(x, norm2_w[layer]), sh2, sc2))
        if not last:
            xc = xc + g1c * (jnp.concatenate([dn_c, at_c], axis=-1) @ w_out[layer])
            xc = xc + g2c * ffn(_modulate(_rmsnorm(xc, norm2_w[layer]), sh2c, sc2c))
    return _rmsnorm(x, final_norm_w)
```

```python
import functools
import math

import jax
import jax.numpy as jnp
from jax import lax
from jax.experimental import pallas as pl
from jax.experimental.pallas import tpu as pltpu

F32 = jnp.float32
BF16 = jnp.bfloat16
EPS = 1e-6

DN_HEADS = 8
HEAD_DIM = 64
DN_DIM = DN_HEADS * HEAD_DIM
DN_CONV_W = 5
DN_CHUNK = 64
AT_Q_HEADS = 8
AT_KV_HEADS = 2
AT_GROUP = AT_Q_HEADS // AT_KV_HEADS
AT_Q_DIM = AT_Q_HEADS * HEAD_DIM
AT_KV_DIM = AT_KV_HEADS * HEAD_DIM
GRID_W = 64
ROPE_THETA = 10000.0
N_EXPERTS = 8

LANES = 128
SUBLANES = 8
MXU_DIM = 256
VMEM_LIMIT = 52 * 1024 * 1024

HEADS_PER_GROUP = MXU_DIM // HEAD_DIM
DN_GROUPS = DN_HEADS // HEADS_PER_GROUP

SEG_QKV = (0, 3 * DN_DIM)
SEG_Z = (3 * DN_DIM, 4 * DN_DIM)
SEG_AB = (4 * DN_DIM, 4 * DN_DIM + LANES)
SEG_ATQ = (SEG_AB[1], SEG_AB[1] + AT_Q_DIM)
SEG_ATK = (SEG_ATQ[1], SEG_ATQ[1] + AT_KV_DIM)
SEG_ATV = (SEG_ATK[1], SEG_ATK[1] + AT_KV_DIM)
IN_PAD = SEG_ATV[1]
SEGS = (SEG_QKV, SEG_Z, SEG_AB, SEG_ATQ, SEG_ATK, SEG_ATV)


def _cparams(sem):
    return pltpu.CompilerParams(dimension_semantics=sem, vmem_limit_bytes=VMEM_LIMIT)


def _bdot(a, b):
    return jnp.dot(a.astype(BF16), b.astype(BF16), preferred_element_type=F32)


def _bdot_nt(a, b):
    return lax.dot_general(a.astype(BF16), b.astype(BF16), (((1,), (1,)), ((), ())),
                           preferred_element_type=F32)


def _bdot_tn(a, b):
    return lax.dot_general(a.astype(BF16), b.astype(BF16), (((0,), (0,)), ((), ())),
                           preferred_element_type=F32)


def _dot_sel(x, sel):
    hi = x.astype(BF16)
    r1 = x - hi.astype(F32)
    mid = r1.astype(BF16)
    lo = (r1 - mid.astype(F32)).astype(BF16)
    dot = functools.partial(jnp.dot, preferred_element_type=F32)
    return dot(hi, sel) + dot(mid, sel) + dot(lo, sel)


def _sigmoid(x):
    return 1.0 / (1.0 + jnp.exp(-x))


def _silu(x):
    return x * _sigmoid(x)


def _norm_modulate(x, nw, shift, scale):
    y = x * lax.rsqrt(jnp.mean(x * x, axis=-1, keepdims=True) + EPS)
    return (y * nw) * (1.0 + scale) + shift


def _pick(n, options):
    for o in options:
        if n % o == 0:
            return o
    raise ValueError(f"no tile in {options} divides {n}")


def _mod_kernel(c_ref, w_ref, b_ref, o_ref):
    o_ref[...] = _bdot(_silu(c_ref[...]), w_ref[...]) + b_ref[...]


def _modulation(c_rows, w_mod, b_mod):
    nl, d, n6 = w_mod.shape
    rows = c_rows.shape[0]
    tn = _pick(n6, (1536, 1024, 512, 256, 128))
    return pl.pallas_call(
        _mod_kernel,
        grid=(nl, n6 // tn),
        in_specs=[pl.BlockSpec((rows, d), lambda l, j: (0, 0)),
                  pl.BlockSpec((None, d, tn), lambda l, j: (l, 0, j)),
                  pl.BlockSpec((None, 1, tn), lambda l, j: (l, 0, j))],
        out_specs=pl.BlockSpec((None, rows, tn), lambda l, j: (l, 0, j)),
        out_shape=jax.ShapeDtypeStruct((nl, rows, n6), F32),
        compiler_params=_cparams(("parallel", "parallel")),
        name="adaln_modulation",
    )(c_rows, w_mod, b_mod.reshape(nl, 1, n6))


def _inproj_kernel(x_ref, mod_ref, nw_ref, w_ref, *out_refs, d):
    h = _norm_modulate(x_ref[...], nw_ref[...], mod_ref[:, 0:d], mod_ref[:, d:2 * d]).astype(BF16)
    for (a, b), o_ref in zip(SEGS, out_refs):
        o_ref[...] = jnp.dot(h, w_ref[:, a:b], preferred_element_type=F32)


def _in_projection(x, mod, nw, w_pad):
    b, t, d = x.shape
    tm = _pick(t, (512, 256, 128))
    return pl.pallas_call(
        functools.partial(_inproj_kernel, d=d),
        grid=(b, t // tm),
        in_specs=[pl.BlockSpec((None, tm, d), lambda i, j: (i, j, 0)),
                  pl.BlockSpec((None, 1, 6 * d), lambda i, j: (i, 0, 0)),
                  pl.BlockSpec((1, d), lambda i, j: (0, 0)),
                  pl.BlockSpec((d, IN_PAD), lambda i, j: (0, 0))],
        out_specs=[pl.BlockSpec((None, tm, s1 - s0), lambda i, j: (i, j, 0)) for s0, s1 in SEGS],
        out_shape=[jax.ShapeDtypeStruct((b, t, s1 - s0), F32) for s0, s1 in SEGS],
        compiler_params=_cparams(("parallel", "parallel")),
        name="norm_in_projection",
    )(x, mod, nw, w_pad)


def _dn_prep_kernel(xp_ref, x_ref, xn_ref, ab_ref, cw_ref, gp_ref, head_ones_ref, cum_f_ref, cum_b_ref,
                    expand_ref, q_ref, k_ref, v_ref, bf_ref, gf_ref, bb_ref, gb_ref, gc_ref, xe_ref, *, tm):
    i = pl.program_id(1)
    halo = SUBLANES
    pad = (DN_CONV_W - 1) // 2
    zeros = jnp.zeros((halo, 3 * DN_DIM), F32)
    xe_ref[0:halo, :] = jnp.where(i > 0, xp_ref[...], zeros)
    xe_ref[halo:halo + tm, :] = x_ref[...]
    xe_ref[halo + tm:halo + tm + halo, :] = jnp.where(i < pl.num_programs(1) - 1, xn_ref[...], zeros)
    acc = cw_ref[0:1, :] * xe_ref[halo - pad:halo - pad + tm, :]
    for j in range(1, DN_CONV_W):
        acc = acc + cw_ref[j:j + 1, :] * xe_ref[halo - pad + j:halo - pad + j + tm, :]
    qkv = _silu(acc)
    ones = head_ones_ref[...]
    q = qkv[:, 0:DN_DIM]
    k = qkv[:, DN_DIM:2 * DN_DIM]
    q_ref[...] = q * lax.rsqrt(_dot_sel(q * q, ones) + EPS) * (HEAD_DIM ** -0.5)
    k_ref[...] = k * lax.rsqrt(_dot_sel(k * k, ones) + EPS)
    v_ref[...] = qkv[:, 2 * DN_DIM:3 * DN_DIM]

    ab = ab_ref[...]
    lane = lax.broadcasted_iota(jnp.int32, ab.shape, 1)
    z = ab + gp_ref[1:2, :]
    softplus = jnp.maximum(z, 0.0) + jnp.log(1.0 + jnp.exp(-jnp.abs(z)))
    g = -jnp.exp(gp_ref[0:1, :]) * softplus
    g = jnp.where(lane < 2 * DN_HEADS, g, 0.0)
    gc = jnp.where(lane < DN_HEADS, _dot_sel_left(cum_f_ref[...], g), _dot_sel_left(cum_b_ref[...], g))
    gc_ref[...] = gc
    gates = jnp.where(lane < 2 * DN_HEADS, gc, _sigmoid(ab))
    gh = gates.astype(BF16)
    r1 = gates - gh.astype(F32)
    gm = r1.astype(BF16)
    gl = (r1 - gm.astype(F32)).astype(BF16)
    dot = functools.partial(jnp.dot, preferred_element_type=F32)
    for n, o_ref in enumerate((gf_ref, gb_ref, bf_ref, bb_ref)):
        e = expand_ref[n]
        o_ref[...] = dot(gh, e) + dot(gm, e) + dot(gl, e)


def _dot_sel_left(sel, x):
    hi = x.astype(BF16)
    r1 = x - hi.astype(F32)
    mid = r1.astype(BF16)
    lo = (r1 - mid.astype(F32)).astype(BF16)
    dot = functools.partial(jnp.dot, preferred_element_type=F32)
    return dot(sel, hi) + dot(sel, mid) + dot(sel, lo)


def _head_ones(width):
    idx = jnp.arange(width) // HEAD_DIM
    return (idx[:, None] == idx[None, :]).astype(BF16)


def _dn_prepare(qkv, ab, conv_w, a_log, dt_bias):
    b, t, c = qkv.shape
    tm = _pick(t, (256, 128, 64))
    nt = t // tm
    hb = tm // SUBLANES
    last8 = t // SUBLANES - 1
    gp = jnp.zeros((SUBLANES, LANES), F32)
    gp = gp.at[0, :2 * DN_HEADS].set(a_log.reshape(-1)).at[1, :2 * DN_HEADS].set(dt_bias.reshape(-1))
    r = jnp.arange(tm)
    same = (r[:, None] // DN_CHUNK) == (r[None, :] // DN_CHUNK)
    cum_f = (same & (r[None, :] <= r[:, None])).astype(BF16)
    cum_b = (same & (r[None, :] >= r[:, None])).astype(BF16)
    col_head = jnp.arange(DN_DIM) // HEAD_DIM
    src = jnp.arange(LANES)[:, None]
    expand = jnp.stack([(src == col_head[None, :] + off) for off in
                        (0, DN_HEADS, 2 * DN_HEADS, 3 * DN_HEADS)]).astype(BF16)
    wide = pl.BlockSpec((None, tm, DN_DIM), lambda i, j: (i, j, 0))
    const = lambda shape: pl.BlockSpec(shape, lambda i, j: (0,) * len(shape))
    outs = pl.pallas_call(
        functools.partial(_dn_prep_kernel, tm=tm),
        grid=(b, nt),
        in_specs=[pl.BlockSpec((None, SUBLANES, c), lambda i, j: (i, jnp.maximum(j * hb - 1, 0), 0)),
                  pl.BlockSpec((None, tm, c), lambda i, j: (i, j, 0)),
                  pl.BlockSpec((None, SUBLANES, c), lambda i, j: (i, jnp.minimum((j + 1) * hb, last8), 0)),
                  pl.BlockSpec((None, tm, LANES), lambda i, j: (i, j, 0)),
                  const((DN_CONV_W, c)), const((SUBLANES, LANES)), const((DN_DIM, DN_DIM)),
                  const((tm, tm)), const((tm, tm)), const((4, LANES, DN_DIM))],
        out_specs=[wide] * 7 + [pl.BlockSpec((None, tm, LANES), lambda i, j: (i, j, 0))],
        out_shape=[jax.ShapeDtypeStruct((b, t, DN_DIM), F32)] * 7 + [jax.ShapeDtypeStruct((b, t, LANES), F32)],
        scratch_shapes=[pltpu.VMEM((tm + 2 * SUBLANES, c), F32)],
        compiler_params=_cparams(("parallel", "parallel")),
        name="deltanet_prepare",
    )(qkv, qkv, qkv, ab, conv_w, gp, _head_ones(DN_DIM), cum_f, cum_b, expand)
    return outs


def _tile4(x):
    return jnp.concatenate([x] * HEADS_PER_GROUP, axis=0)


def _collapse(full, same_head):
    kept = jnp.where(same_head, full, 0.0)
    out = kept[0:DN_CHUNK]
    for h in range(1, HEADS_PER_GROUP):
        out = out + kept[h * DN_CHUNK:(h + 1) * DN_CHUNK]
    return out


def _delta_chunk(q, k, v, beta, gcx, g_cols, g_row, state, reverse):
    c = DN_CHUNK
    n = HEADS_PER_GROUP * c
    row = lax.broadcasted_iota(jnp.int32, (n, n), 0)
    col = lax.broadcasted_iota(jnp.int32, (n, n), 1)
    same_head = (row // c) == (col // c)
    ri, cj = row % c, col % c
    incl = same_head & ((ri <= cj) if reverse else (ri >= cj))
    strict = same_head & ((ri < cj) if reverse else (ri > cj))
    eye = (row == col).astype(F32)

    g_last = gcx[0:1, :] if reverse else gcx[c - 1:c, :]
    eg = jnp.exp(gcx)
    k_beta = k * beta
    q_dec = q * eg
    k_dec = k * jnp.exp(g_last - gcx)

    col_b = jnp.concatenate([jnp.broadcast_to(gc, (c, n)) for gc in g_cols], axis=0)
    diff = col_b - jnp.broadcast_to(g_row, (n, n))
    decay = jnp.where(incl, jnp.exp(jnp.where(incl, diff, 0.0)), 0.0)

    k_t = _tile4(k)
    a = jnp.where(strict, _bdot_nt(jnp.where(same_head, _tile4(k_beta), 0.0), k_t) * decay, 0.0)
    attn = _bdot_nt(jnp.where(same_head, _tile4(q), 0.0), k_t) * decay

    def joins(s):
        return ((row // (2 * s)) == (col // (2 * s))) & ((row // s) != (col // s))

    inv = eye - jnp.where(joins(1), a, 0.0)
    s = 2
    while s < c:
        inv = inv - _bdot(inv, _bdot(jnp.where(joins(s), a, 0.0), inv))
        s *= 2

    rhs = jnp.concatenate([_tile4(v * beta), _tile4(k_beta * eg)], axis=1)
    sol = _bdot(inv, rhs)
    u = _collapse(sol[:, 0:n], same_head)
    w = _collapse(sol[:, n:2 * n], same_head)

    ws = _bdot(jnp.concatenate([w, q_dec], axis=0), state)
    v_new = u - ws[0:c]
    o = ws[c:2 * c] + _collapse(_bdot(attn, _tile4(v_new)), same_head)
    new_state = state * jnp.exp(g_last) + jnp.where(same_head, _bdot_tn(k_dec, v_new), 0.0)
    return o, new_state


def _dn_scan_kernel(qf, kf, vf, btf, gxf, gcf, grf, qb, kb, vb, btb, gxb, gcb, grb, s0_ref,
                    of_ref, ob_ref, sfin_ref, s_ref):
    step = pl.program_id(1)

    @pl.when(step == 0)
    def _():
        s_ref[...] = s0_ref[...]

    dirs = ((qf, kf, vf, btf, gxf, gcf, grf, of_ref), (qb, kb, vb, btb, gxb, gcb, grb, ob_ref))
    for d, (q, k, v, bt, gx, gc, gr, o_ref) in enumerate(dirs):
        for g in range(DN_GROUPS):
            sl = slice(g * MXU_DIM, (g + 1) * MXU_DIM)
            lane0 = d * DN_HEADS + g * HEADS_PER_GROUP
            g_cols = [gc[:, lane0 + h:lane0 + h + 1] for h in range(HEADS_PER_GROUP)]
            o, s_new = _delta_chunk(q[:, sl], k[:, sl], v[:, sl], bt[:, sl], gx[:, sl], g_cols,
                                    gr[:, sl], s_ref[d, g], reverse=(d == 1))
            o_ref[:, sl] = o
            s_ref[d, g] = s_new

    @pl.when(step == pl.num_programs(1) - 1)
    def _():
        sfin_ref[...] = s_ref[...]


def _dn_scan(prep, s0):
    q, k, v, beta_f, gcx_f, beta_b, gcx_b, gc = prep
    b, t, _ = q.shape
    n = t // DN_CHUNK
    g_rows = gc[:, :, :2 * DN_HEADS].reshape(b, n, DN_CHUNK, 2, DN_HEADS)
    g_rows = g_rows.transpose(0, 3, 1, 4, 2).reshape(b, 2, n, 1, DN_DIM)
    fwd = lambda i, j: (i, j, 0)
    bwd = lambda i, j: (i, n - 1 - j, 0)
    wide = lambda im: pl.BlockSpec((None, DN_CHUNK, DN_DIM), im)
    lanes = lambda im: pl.BlockSpec((None, DN_CHUNK, LANES), im)
    grow = lambda d, im: pl.BlockSpec((None, None, None, 1, DN_DIM),
                                      lambda i, j: (i, d, im(i, j)[1], 0, 0))
    state = pl.BlockSpec((None, 2, DN_GROUPS, MXU_DIM, MXU_DIM), lambda i, j: (i, 0, 0, 0, 0))
    o_f, o_b, s_fin = pl.pallas_call(
        _dn_scan_kernel,
        grid=(b, n),
        in_specs=[wide(fwd)] * 5 + [lanes(fwd), grow(0, fwd)] + [wide(bwd)] * 5 + [lanes(bwd), grow(1, bwd)]
        + [state],
        out_specs=[wide(fwd), wide(bwd), state],
        out_shape=[jax.ShapeDtypeStruct((b, t, DN_DIM), F32)] * 2
        + [jax.ShapeDtypeStruct((b, 2, DN_GROUPS, MXU_DIM, MXU_DIM), F32)],
        scratch_shapes=[pltpu.VMEM((2, DN_GROUPS, MXU_DIM, MXU_DIM), F32)],
        compiler_params=_cparams(("parallel", "arbitrary")),
        name="deltanet_scan",
    )(q, k, v, beta_f, gcx_f, gc, g_rows, q, k, v, beta_b, gcx_b, gc, g_rows, s0)
    return o_f, o_b, s_fin


def _rope(x, cos_t, sin_t):
    width = x.shape[-1]
    half = HEAD_DIM // 2
    lane = lax.broadcasted_iota(jnp.int32, x.shape, 1)
    partner = jnp.where(lane % HEAD_DIM < half, pltpu.roll(x, width - half, 1), pltpu.roll(x, half, 1))
    reps = width // LANES
    cos_w = jnp.concatenate([cos_t] * reps, axis=1) if reps > 1 else cos_t
    sin_w = jnp.concatenate([sin_t] * reps, axis=1) if reps > 1 else sin_t
    return x * cos_w + partner * sin_w


def _head_rmsnorm(x, ones, w):
    ms = _dot_sel(x * x, ones) * (1.0 / HEAD_DIM)
    return x * lax.rsqrt(ms + EPS) * w


def _attn_prep_kernel(q_ref, k_ref, v_ref, qw_ref, kw_ref, cos_ref, sin_ref, ones_q_ref, ones_k_ref,
                      qo_ref, ko_ref, vo_ref, *, rope):
    q = _head_rmsnorm(q_ref[...], ones_q_ref[...], qw_ref[...])
    k = _head_rmsnorm(k_ref[...], ones_k_ref[...], kw_ref[...])
    if rope:
        q = _rope(q, cos_ref[...], sin_ref[...])
        k = _rope(k, cos_ref[...], sin_ref[...])
    qo_ref[...] = (q * (HEAD_DIM ** -0.5)).astype(BF16)
    v = v_ref[...]
    lane = lax.broadcasted_iota(jnp.int32, k.shape, 1)
    first = lane < HEAD_DIM
    k_sw = pltpu.roll(k, HEAD_DIM, 1)
    v_sw = pltpu.roll(v, HEAD_DIM, 1)
    ko_ref[0] = jnp.where(first, k, k_sw).astype(BF16)
    ko_ref[1] = jnp.where(first, k_sw, k).astype(BF16)
    vo_ref[0] = jnp.where(first, v, v_sw).astype(BF16)
    vo_ref[1] = jnp.where(first, v_sw, v).astype(BF16)


def _rope_tables(t):
    rows = t // GRID_W
    row_pos = jnp.repeat(jnp.arange(rows, dtype=F32), GRID_W, total_repeat_length=t)
    col_pos = jnp.tile(jnp.arange(GRID_W, dtype=F32), rows)
    n_freq = HEAD_DIM // 4
    freqs = ROPE_THETA ** (-jnp.arange(n_freq, dtype=F32) / n_freq)
    ang = jnp.concatenate([row_pos[:, None] * freqs, col_pos[:, None] * freqs], axis=-1)
    cos, sin = jnp.cos(ang), jnp.sin(ang)
    reps = LANES // HEAD_DIM
    return (jnp.tile(jnp.concatenate([cos, cos], axis=-1), (1, reps)),
            jnp.tile(jnp.concatenate([-sin, sin], axis=-1), (1, reps)))


def _attn_prepare(atq, atk, atv, qn_w, kn_w, rope):
    b, t, _ = atq.shape
    tm = _pick(t, (512, 256, 128))
    if rope:
        cos_t, sin_t = _rope_tables(t)
    else:
        cos_t = sin_t = jnp.zeros((t, LANES), F32)
    qw = jnp.tile(qn_w, AT_Q_HEADS).reshape(1, AT_Q_DIM)
    kw = jnp.tile(kn_w, AT_KV_HEADS).reshape(1, AT_KV_DIM)
    const = lambda shape: pl.BlockSpec(shape, lambda i, j: (0,) * len(shape))
    tab = pl.BlockSpec((tm, LANES), lambda i, j: (j, 0))
    kv_out = pl.BlockSpec((None, AT_KV_HEADS, tm, AT_KV_DIM), lambda i, j: (i, 0, j, 0))
    return pl.pallas_call(
        functools.partial(_attn_prep_kernel, rope=rope),
        grid=(b, t // tm),
        in_specs=[pl.BlockSpec((None, tm, AT_Q_DIM), lambda i, j: (i, j, 0)),
                  pl.BlockSpec((None, tm, AT_KV_DIM), lambda i, j: (i, j, 0)),
                  pl.BlockSpec((None, tm, AT_KV_DIM), lambda i, j: (i, j, 0)),
                  const((1, AT_Q_DIM)), const((1, AT_KV_DIM)), tab, tab,
                  const((AT_Q_DIM, AT_Q_DIM)), const((AT_KV_DIM, AT_KV_DIM))],
        out_specs=[pl.BlockSpec((None, tm, AT_Q_DIM), lambda i, j: (i, j, 0)), kv_out, kv_out],
        out_shape=[jax.ShapeDtypeStruct((b, t, AT_Q_DIM), BF16),
                   jax.ShapeDtypeStruct((b, AT_KV_HEADS, t, AT_KV_DIM), BF16),
                   jax.ShapeDtypeStruct((b, AT_KV_HEADS, t, AT_KV_DIM), BF16)],
        compiler_params=_cparams(("parallel", "parallel")),
        name="attention_prepare",
    )(atq, atk, atv, qw, kw, cos_t, sin_t, _head_ones(AT_Q_DIM), _head_ones(AT_KV_DIM))


def _flash_kernel(q_ref, k_ref, v_ref, o_ref, qs_ref, m_ref, l_ref, acc_ref, *, tq):
    j = pl.program_id(3)

    @pl.when(j == 0)
    def _():
        q = q_ref[...]
        lane = lax.broadcasted_iota(jnp.int32, q.shape, 1)
        for h in range(AT_GROUP):
            qs_ref[h * tq:(h + 1) * tq, :] = jnp.where(lane // HEAD_DIM == h, q, jnp.zeros_like(q))
        m_ref[...] = jnp.full(m_ref.shape, -jnp.inf, F32)
        l_ref[...] = jnp.zeros(l_ref.shape, F32)
        acc_ref[...] = jnp.zeros(acc_ref.shape, F32)

    k = k_ref[...]
    k2 = jnp.concatenate([k, k], axis=1)
    s = lax.dot_general(qs_ref[...], k2, (((1,), (1,)), ((), ())), preferred_element_type=F32)
    m_prev = m_ref[...]
    m_next = jnp.maximum(m_prev, jnp.max(s, axis=1, keepdims=True))
    p = jnp.exp(s - m_next[:, 0:1])
    alpha = jnp.exp(m_prev - m_next)
    l_ref[...] = alpha * l_ref[...] + jnp.sum(p, axis=1, keepdims=True)
    acc_ref[...] = alpha * acc_ref[...] + jnp.dot(p.astype(BF16), v_ref[...], preferred_element_type=F32)
    m_ref[...] = m_next

    @pl.when(j == pl.num_programs(3) - 1)
    def _():
        o = acc_ref[...] / l_ref[...]
        lane = lax.broadcasted_iota(jnp.int32, (tq, LANES), 1)
        first = lane < HEAD_DIM
        pairs = [jnp.where(first, o[(2 * p2) * tq:(2 * p2 + 1) * tq], o[(2 * p2 + 1) * tq:(2 * p2 + 2) * tq])
                 for p2 in range(AT_GROUP // 2)]
        o_ref[...] = jnp.concatenate(pairs, axis=1)


def _flash_attention(q, kd, vd):
    b, t, _ = q.shape
    s = kd.shape[2]
    tq = _pick(t, (256, 128))
    tk = _pick(s, (768, 512, 256, 128))
    gw = AT_GROUP * HEAD_DIM
    return pl.pallas_call(
        functools.partial(_flash_kernel, tq=tq),
        grid=(b, AT_KV_HEADS, t // tq, s // tk),
        in_specs=[pl.BlockSpec((None, tq, gw), lambda i, g, a, j: (i, a, g)),
                  pl.BlockSpec((None, None, tk, AT_KV_DIM), lambda i, g, a, j: (i, g, j, 0)),
                  pl.BlockSpec((None, None, tk, AT_KV_DIM), lambda i, g, a, j: (i, g, j, 0))],
        out_specs=pl.BlockSpec((None, tq, gw), lambda i, g, a, j: (i, a, g)),
        out_shape=jax.ShapeDtypeStruct((b, t, AT_Q_DIM), F32),
        scratch_shapes=[pltpu.VMEM((AT_GROUP * tq, gw), BF16),
                        pltpu.VMEM((AT_GROUP * tq, LANES), F32),
                        pltpu.VMEM((AT_GROUP * tq, LANES), F32),
                        pltpu.VMEM((AT_GROUP * tq, LANES), F32)],
        compiler_params=_cparams(("parallel", "parallel", "parallel", "arbitrary")),
        name="gqa_flash_attention",
    )(q, kd, vd)


def _outproj_kernel(of_ref, ob_ref, z_ref, at_ref, x_ref, mod_ref, nw_ref, ones_ref, w_ref, o_ref, *, d):
    o = of_ref[...] + ob_ref[...]
    ms = _dot_sel(o * o, ones_ref[...]) * (1.0 / HEAD_DIM)
    dn = o * lax.rsqrt(ms + EPS) * nw_ref[...] * _silu(z_ref[...])
    mix = _bdot(dn, w_ref[0:DN_DIM, :]) + _bdot(at_ref[...], w_ref[DN_DIM:DN_DIM + AT_Q_DIM, :])
    o_ref[...] = x_ref[...] + mod_ref[:, 2 * d:3 * d] * mix


def _out_projection(o_f, o_b, z, at, x, mod, dn_norm_w, w_out):
    b, t, d = x.shape
    tm = _pick(t, (512, 256, 128))
    half = pl.BlockSpec((None, tm, DN_DIM), lambda i, j: (i, j, 0))
    full = pl.BlockSpec((None, tm, d), lambda i, j: (i, j, 0))
    const = lambda shape: pl.BlockSpec(shape, lambda i, j: (0,) * len(shape))
    return pl.pallas_call(
        functools.partial(_outproj_kernel, d=d),
        grid=(b, t // tm),
        in_specs=[half, half, half, half, full, pl.BlockSpec((None, 1, 6 * d), lambda i, j: (i, 0, 0)),
                  const((1, DN_DIM)), const((DN_DIM, DN_DIM)), const((DN_DIM + AT_Q_DIM, d))],
        out_specs=full,
        out_shape=jax.ShapeDtypeStruct((b, t, d), F32),
        compiler_params=_cparams(("parallel", "parallel")),
        name="mix_out_projection",
    )(o_f, o_b, z, at, x, mod, jnp.tile(dn_norm_w, DN_HEADS).reshape(1, DN_DIM), _head_ones(DN_DIM), w_out)


def _ffn_kernel(x_ref, mod_ref, nw_ref, fw_ref, wg_ref, wu_ref, wd_ref, o_ref, h_ref, acc_ref, *, d, final):
    f = pl.program_id(2)

    @pl.when(f == 0)
    def _():
        h_ref[...] = _norm_modulate(x_ref[...], nw_ref[...], mod_ref[:, 3 * d:4 * d],
                                    mod_ref[:, 4 * d:5 * d]).astype(BF16)
        acc_ref[...] = jnp.zeros(acc_ref.shape, F32)

    h = h_ref[...]
    gate = jnp.dot(h, wg_ref[...], preferred_element_type=F32)
    up = jnp.dot(h, wu_ref[...], preferred_element_type=F32)
    acc_ref[...] += jnp.dot((_silu(gate) * up).astype(BF16), wd_ref[...], preferred_element_type=F32)

    @pl.when(f == pl.num_programs(2) - 1)
    def _():
        y = x_ref[...] + mod_ref[:, 5 * d:6 * d] * acc_ref[...]
        if final:
            y = y * lax.rsqrt(jnp.mean(y * y, axis=-1, keepdims=True) + EPS) * fw_ref[...]
        o_ref[...] = y


def _dense_ffn(x, mod, nw, final_w, wg, wu, wd, final):
    b, t, d = x.shape
    fdim = wg.shape[1]
    tm = _pick(t, (1024, 512, 256, 128))
    tf = _pick(fdim, (512, 256, 128))
    full = pl.BlockSpec((None, tm, d), lambda i, j, f: (i, j, 0))
    const = lambda shape: pl.BlockSpec(shape, lambda i, j, f: (0,) * len(shape))
    return pl.pallas_call(
        functools.partial(_ffn_kernel, d=d, final=final),
        grid=(b, t // tm, fdim // tf),
        in_specs=[full, pl.BlockSpec((None, 1, 6 * d), lambda i, j, f: (i, 0, 0)), const((1, d)), const((1, d)),
                  pl.BlockSpec((d, tf), lambda i, j, f: (0, f)),
                  pl.BlockSpec((d, tf), lambda i, j, f: (0, f)),
                  pl.BlockSpec((tf, d), lambda i, j, f: (f, 0))],
        out_specs=full,
        out_shape=jax.ShapeDtypeStruct((b, t, d), F32),
        scratch_shapes=[pltpu.VMEM((tm, d), BF16), pltpu.VMEM((tm, d), F32)],
        compiler_params=_cparams(("parallel", "parallel", "arbitrary")),
        name="dense_swiglu",
    )(x, mod, nw, final_w, wg, wu, wd)


R_E1, R_E2, R_W1, R_W2, R_RANK1, R_RANK2 = range(6)


def _router_kernel(x_ref, mod_ref, nw_ref, rw_ref, rb_ref, tri_ref, h_ref, route_ref, cnt_ref, carry_ref, *, d):
    @pl.when(pl.program_id(0) == 0)
    def _():
        carry_ref[...] = jnp.zeros(carry_ref.shape, F32)

    h = _norm_modulate(x_ref[...], nw_ref[...], mod_ref[:, 3 * d:4 * d], mod_ref[:, 4 * d:5 * d])
    h_ref[...] = h
    logits = _bdot(h, rw_ref[...]) + rb_ref[...]
    lane = lax.broadcasted_iota(jnp.int32, logits.shape, 1)
    neg = jnp.float32(-jnp.inf)
    lg = jnp.where(lane < N_EXPERTS, logits, neg)
    m1 = jnp.max(lg, axis=1, keepdims=True)
    i1 = jnp.min(jnp.where(lg == m1, lane, LANES), axis=1, keepdims=True)
    hot1 = lane == i1
    lg2 = jnp.where(hot1, neg, lg)
    m2 = jnp.max(lg2, axis=1, keepdims=True)
    i2 = jnp.min(jnp.where(lg2 == m2, lane, LANES), axis=1, keepdims=True)
    hot2 = lane == i2
    e2 = jnp.exp(m2 - m1)
    w1 = 1.0 / (1.0 + e2)
    w2 = e2 / (1.0 + e2)
    member = jnp.where(hot1 | hot2, 1.0, 0.0)
    before = jnp.dot(tri_ref[...], member.astype(BF16), preferred_element_type=F32) + carry_ref[0:1, :]
    rank1 = jnp.sum(jnp.where(hot1, before, 0.0), axis=1, keepdims=True)
    rank2 = jnp.sum(jnp.where(hot2, before, 0.0), axis=1, keepdims=True)
    rec = jnp.zeros(logits.shape, F32)
    for ln, val in ((R_E1, i1.astype(F32)), (R_E2, i2.astype(F32)), (R_W1, w1), (R_W2, w2),
                    (R_RANK1, rank1), (R_RANK2, rank2)):
        rec = jnp.where(lane == ln, val, rec)
    route_ref[...] = rec
    tb = member.shape[0]
    total = before[tb - 1:tb, :] + member[tb - 1:tb, :]
    carry_ref[...] = jnp.broadcast_to(total, carry_ref.shape)
    cnt_ref[...] = jnp.broadcast_to(total, cnt_ref.shape)


def _route(x2, mod, nw, router_w, router_b, rows_per_mod):
    n, d = x2.shape
    tb = _pick(rows_per_mod, (512, 256, 128))
    per = rows_per_mod // tb
    rw = jnp.zeros((d, LANES), BF16).at[:, :N_EXPERTS].set(router_w.astype(BF16))
    rb = jnp.zeros((1, LANES), F32).at[0, :N_EXPERTS].set(router_b)
    r = jnp.arange(tb)
    tri = (r[None, :] < r[:, None]).astype(BF16)
    const = lambda shape: pl.BlockSpec(shape, lambda i: (0,) * len(shape))
    return pl.pallas_call(
        functools.partial(_router_kernel, d=d),
        grid=(n // tb,),
        in_specs=[pl.BlockSpec((tb, d), lambda i: (i, 0)),
                  pl.BlockSpec((None, 1, 6 * d), lambda i: (i // per, 0, 0)),
                  const((1, d)), const((d, LANES)), const((1, LANES)), const((tb, tb))],
        out_specs=[pl.BlockSpec((tb, d), lambda i: (i, 0)), pl.BlockSpec((tb, LANES), lambda i: (i, 0)),
                   const((SUBLANES, LANES))],
        out_shape=[jax.ShapeDtypeStruct((n, d), F32), jax.ShapeDtypeStruct((n, LANES), F32),
                   jax.ShapeDtypeStruct((SUBLANES, LANES), F32)],
        scratch_shapes=[pltpu.VMEM((SUBLANES, LANES), F32)],
        compiler_params=_cparams(("arbitrary",)),
        name="moe_router_top2",
    )(x2, mod, nw, rw, rb, tri)


def _dispatch_kernel(dest_ref, h_ref, xs_in_ref, xs_ref, sem, *, tb):
    del xs_in_ref
    base = pl.program_id(0) * tb

    def copy(t, kk):
        return pltpu.make_async_copy(h_ref.at[pl.ds(base + t, 1), :],
                                     xs_ref.at[pl.ds(dest_ref[0, 2 * t + kk], 1), :], sem)

    def start(t, c):
        copy(t, 0).start()
        copy(t, 1).start()
        return c

    def wait(t, c):
        copy(t, 0).wait()
        copy(t, 1).wait()
        return c

    lax.fori_loop(0, tb, start, 0)
    lax.fori_loop(0, tb, wait, 0)


def _dispatch(h, dest, p_rows):
    n, d = h.shape
    tb = _pick(n, (512, 256, 128))
    return pl.pallas_call(
        functools.partial(_dispatch_kernel, tb=tb),
        grid=(n // tb,),
        in_specs=[pl.BlockSpec((None, 1, 2 * tb), lambda i: (i, 0, 0), memory_space=pltpu.SMEM),
                  pl.BlockSpec(memory_space=pl.ANY), pl.BlockSpec(memory_space=pl.ANY)],
        out_specs=pl.BlockSpec(memory_space=pl.ANY),
        out_shape=jax.ShapeDtypeStruct((p_rows, d), F32),
        scratch_shapes=[pltpu.SemaphoreType.DMA(())],
        input_output_aliases={2: 0},
        compiler_params=_cparams(("arbitrary",)),
        name="moe_dispatch_rows",
    )(dest.reshape(n // tb, 1, 2 * tb), h, jnp.zeros((p_rows, d), F32))


def _expert_kernel(be_ref, bx_ref, bv_ref, x_ref, wg_ref, wu_ref, wd_ref, o_ref, h_ref, acc_ref):
    i, f = pl.program_id(0), pl.program_id(1)
    valid = bv_ref[i] == 1
    last = f == pl.num_programs(1) - 1

    @pl.when(valid & (f == 0))
    def _():
        h_ref[...] = x_ref[...].astype(BF16)
        acc_ref[...] = jnp.zeros(acc_ref.shape, F32)

    @pl.when(valid)
    def _():
        h = h_ref[...]
        gate = jnp.dot(h, wg_ref[...], preferred_element_type=F32)
        up = jnp.dot(h, wu_ref[...], preferred_element_type=F32)
        acc_ref[...] += jnp.dot((_silu(gate) * up).astype(BF16), wd_ref[...], preferred_element_type=F32)

    @pl.when(valid & last)
    def _():
        o_ref[...] = acc_ref[...]

    @pl.when(jnp.logical_not(valid) & last)
    def _():
        o_ref[...] = jnp.zeros(o_ref.shape, F32)


def _experts(xs, blk_e, blk_x, blk_v, wg, wu, wd, mb):
    p_rows, d = xs.shape
    fdim = wg.shape[2]
    tf = _pick(fdim, (512, 256, 128))
    nf = fdim // tf
    fsel = lambda i, f, bv: jnp.where(bv[i] == 1, f, nf - 1)
    grid_spec = pltpu.PrefetchScalarGridSpec(
        num_scalar_prefetch=3,
        grid=(p_rows // mb, nf),
        in_specs=[pl.BlockSpec((mb, d), lambda i, f, be, bx, bv: (bx[i], 0)),
                  pl.BlockSpec((None, d, tf), lambda i, f, be, bx, bv: (be[i], 0, fsel(i, f, bv))),
                  pl.BlockSpec((None, d, tf), lambda i, f, be, bx, bv: (be[i], 0, fsel(i, f, bv))),
                  pl.BlockSpec((None, tf, d), lambda i, f, be, bx, bv: (be[i], fsel(i, f, bv), 0))],
        out_specs=pl.BlockSpec((mb, d), lambda i, f, be, bx, bv: (i, 0)),
        scratch_shapes=[pltpu.VMEM((mb, d), BF16), pltpu.VMEM((mb, d), F32)])
    return pl.pallas_call(
        _expert_kernel,
        grid_spec=grid_spec,
        out_shape=jax.ShapeDtypeStruct((p_rows, d), F32),
        compiler_params=_cparams(("arbitrary", "arbitrary")),
        name="moe_expert_swiglu",
    )(blk_e, blk_x, blk_v, xs, wg, wu, wd)


def _combine_kernel(dest_ref, ys_ref, x_ref, route_ref, mod_ref, fw_ref, o_ref, y1_ref, y2_ref, sem, *,
                    d, tb, final):
    def copy(t, kk, buf):
        return pltpu.make_async_copy(ys_ref.at[pl.ds(dest_ref[0, 2 * t + kk], 1), :],
                                     buf.at[pl.ds(t, 1), :], sem)

    def start(t, c):
        copy(t, 0, y1_ref).start()
        copy(t, 1, y2_ref).start()
        return c

    def wait(t, c):
        copy(t, 0, y1_ref).wait()
        copy(t, 1, y2_ref).wait()
        return c

    lax.fori_loop(0, tb, start, 0)
    lax.fori_loop(0, tb, wait, 0)
    w1 = route_ref[:, R_W1:R_W1 + 1]
    w2 = route_ref[:, R_W2:R_W2 + 1]
    y = x_ref[...] + mod_ref[:, 5 * d:6 * d] * (y1_ref[...] * w1 + y2_ref[...] * w2)
    if final:
        y = y * lax.rsqrt(jnp.mean(y * y, axis=-1, keepdims=True) + EPS) * fw_ref[...]
    o_ref[...] = y


def _combine(ys, dest, x2, route, mod, final_w, rows_per_mod, final):
    n, d = x2.shape
    tb = _pick(rows_per_mod, (256, 128))
    per = rows_per_mod // tb
    return pl.pallas_call(
        functools.partial(_combine_kernel, d=d, tb=tb, final=final),
        grid=(n // tb,),
        in_specs=[pl.BlockSpec((None, 1, 2 * tb), lambda i: (i, 0, 0), memory_space=pltpu.SMEM),
                  pl.BlockSpec(memory_space=pl.ANY),
                  pl.BlockSpec((tb, d), lambda i: (i, 0)),
                  pl.BlockSpec((tb, LANES), lambda i: (i, 0)),
                  pl.BlockSpec((None, 1, 6 * d), lambda i: (i // per, 0, 0)),
                  pl.BlockSpec((1, d), lambda i: (0, 0))],
        out_specs=pl.BlockSpec((tb, d), lambda i: (i, 0)),
        out_shape=jax.ShapeDtypeStruct((n, d), F32),
        scratch_shapes=[pltpu.VMEM((tb, d), F32), pltpu.VMEM((tb, d), F32), pltpu.SemaphoreType.DMA(())],
        compiler_params=_cparams(("arbitrary",)),
        name="moe_combine_rows",
    )(dest.reshape(n // tb, 1, 2 * tb), ys, x2, route, mod, final_w)


def _moe_ffn(x, mod, nw, final_w, router_w, router_b, wg, wu, wd, final):
    b, t, d = x.shape
    n = b * t
    x2 = x.reshape(n, d)
    h, route, counts = _route(x2, mod, nw, router_w, router_b, t)
    mb = _pick(n, (1024, 512, 256, 128))
    cnt = counts[0, :N_EXPERTS].astype(jnp.int32)
    nblk_e = (cnt + mb - 1) // mb
    blk_end = jnp.cumsum(nblk_e)
    pad_start = (blk_end - nblk_e) * mb
    nblk = (2 * n) // mb + N_EXPERTS
    bi = jnp.arange(nblk, dtype=jnp.int32)
    total_blk = blk_end[-1]
    blk_v = (bi < total_blk).astype(jnp.int32)
    clamped = jnp.minimum(bi, total_blk - 1)
    blk_e = jnp.minimum(jnp.searchsorted(blk_end, clamped, side='right'), N_EXPERTS - 1).astype(jnp.int32)
    e12 = route[:, R_E1:R_E2 + 1].astype(jnp.int32)
    rank12 = route[:, R_RANK1:R_RANK2 + 1].astype(jnp.int32)
    dest = (pad_start[e12] + rank12).reshape(n * 2)
    xs = _dispatch(h, dest, nblk * mb)
    ys = _experts(xs, blk_e, clamped.astype(jnp.int32), blk_v, wg, wu, wd, mb)
    out = _combine(ys, dest, x2, route, mod, final_w, t, final)
    return out.reshape(b, t, d)


def _pad_in_weight(w):
    d = w.shape[0]
    ab_end = 4 * DN_DIM + 4 * DN_HEADS
    return jnp.concatenate([w[:, :ab_end], jnp.zeros((d, SEG_AB[1] - ab_end), w.dtype), w[:, ab_end:]],
                           axis=1).astype(BF16)


def kernel(x, c, ctx, c_ctx, w_mod, b_mod, norm1_w, norm2_w, w_in, dn_conv_w, dn_a_log, dn_dt_bias, dn_norm_w, at_qnorm_w, at_knorm_w, w_out, ffn_w_gate, ffn_w_up, ffn_w_down, moe_router_w, moe_router_b, moe_w_gate, moe_w_up, moe_w_down, final_norm_w):
    b, t, d = x.shape
    depth = w_mod.shape[0]
    xc = ctx
    rows = -(-(b + 1) // SUBLANES) * SUBLANES
    c_rows = jnp.zeros((rows, d), F32).at[:b].set(c).at[b].set(c_ctx)
    mod_all = _modulation(c_rows, w_mod, b_mod)
    final_w = final_norm_w.reshape(1, d)
    zero_state = jnp.zeros((b, 2, DN_GROUPS, MXU_DIM, MXU_DIM), F32)

    for layer in range(depth):
        last = layer == depth - 1
        mod = mod_all[layer, :b].reshape(b, 1, 6 * d)
        mod_c = jnp.broadcast_to(mod_all[layer, b].reshape(1, 1, 6 * d), (b, 1, 6 * d))
        n1 = norm1_w[layer].reshape(1, d)
        n2 = norm2_w[layer].reshape(1, d)
        w_pad = _pad_in_weight(w_in[layer])
        w_o = w_out[layer].astype(BF16)

        qkv, z, ab, atq, atk, atv = _in_projection(x, mod, n1, w_pad)
        qkv_c, z_c, ab_c, atq_c, atk_c, atv_c = _in_projection(xc, mod_c, n1, w_pad)

        prep_c = _dn_prepare(qkv_c, ab_c, dn_conv_w[layer], dn_a_log[layer], dn_dt_bias[layer])
        of_c, ob_c, s_ctx = _dn_scan(prep_c, zero_state)
        prep = _dn_prepare(qkv, ab, dn_conv_w[layer], dn_a_log[layer], dn_dt_bias[layer])
        o_f, o_b, _ = _dn_scan(prep, s_ctx)

        q_l, kd_l, vd_l = _attn_prepare(atq, atk, atv, at_qnorm_w[layer], at_knorm_w[layer], rope=True)
        q_c, kd_c, vd_c = _attn_prepare(atq_c, atk_c, atv_c, at_qnorm_w[layer], at_knorm_w[layer], rope=False)
        at = _flash_attention(q_l, jnp.concatenate([kd_l, kd_c], axis=2), jnp.concatenate([vd_l, vd_c], axis=2))

        x = _out_projection(o_f, o_b, z, at, x, mod, dn_norm_w[layer], w_o)
        if not last:
            at_c = _flash_attention(q_c, kd_c, vd_c)
            xc = _out_projection(of_c, ob_c, z_c, at_c, xc, mod_c, dn_norm_w[layer], w_o)

        if layer % 2 == 0:
            i = layer // 2
            wg, wu, wd = (w[i].astype(BF16) for w in (ffn_w_gate, ffn_w_up, ffn_w_down))
            x = _dense_ffn(x, mod, n2, final_w, wg, wu, wd, last)
            if not last:
                xc = _dense_ffn(xc, mod_c, n2, final_w, wg, wu, wd, False)
        else:
            j = layer // 2
            wg, wu, wd = (w[j].astype(BF16) for w in (moe_w_gate, moe_w_up, moe_w_down))
            x = _moe_ffn(x, mod, n2, final_w, moe_router_w[j], moe_router_b[j], wg, wu, wd, last)
            if not last:
                xc = _moe_ffn(xc, mod_c, n2, final_w, moe_router_w[j], moe_router_b[j], wg, wu, wd, False)
    return x
```

```python
import functools
import math

import jax
import jax.numpy as jnp
from jax import lax
from jax.experimental import pallas as pl
from jax.experimental.pallas import tpu as pltpu

F32 = jnp.float32
BF16 = jnp.bfloat16
EPS = 1e-6
LOG2E = math.log2(math.e)

DN_HEADS = 8
HEAD_DIM = 64
DN_DIM = DN_HEADS * HEAD_DIM
DN_CONV_W = 5
DN_CHUNK = 64
AT_Q_HEADS = 8
AT_KV_HEADS = 2
AT_GROUP = AT_Q_HEADS // AT_KV_HEADS
AT_Q_DIM = AT_Q_HEADS * HEAD_DIM
AT_KV_DIM = AT_KV_HEADS * HEAD_DIM
GRID_W = 64
ROPE_THETA = 10000.0
N_EXPERTS = 8

LANES = 128
SUBLANES = 8
MXU_DIM = 256
VMEM_LIMIT = 52 * 1024 * 1024

HEADS_PER_GROUP = MXU_DIM // HEAD_DIM
DN_GROUPS = DN_HEADS // HEADS_PER_GROUP
FLASH_ROWS = 256

SEG_QKV = (0, 3 * DN_DIM)
SEG_Z = (3 * DN_DIM, 4 * DN_DIM)
SEG_AB = (4 * DN_DIM, 4 * DN_DIM + LANES)
SEG_ATQ = (SEG_AB[1], SEG_AB[1] + AT_Q_DIM)
SEG_ATK = (SEG_ATQ[1], SEG_ATQ[1] + AT_KV_DIM)
SEG_ATV = (SEG_ATK[1], SEG_ATK[1] + AT_KV_DIM)
IN_PAD = SEG_ATV[1]
SEGS = (SEG_QKV, SEG_Z, SEG_AB, SEG_ATQ, SEG_ATK, SEG_ATV)


def _cparams(sem):
    return pltpu.CompilerParams(dimension_semantics=sem, vmem_limit_bytes=VMEM_LIMIT)


def _bdot(a, b):
    return jnp.dot(a.astype(BF16), b.astype(BF16), preferred_element_type=F32)


def _bdot_nt(a, b):
    return lax.dot_general(a.astype(BF16), b.astype(BF16), (((1,), (1,)), ((), ())),
                           preferred_element_type=F32)


def _bdot_tn(a, b):
    return lax.dot_general(a.astype(BF16), b.astype(BF16), (((0,), (0,)), ((), ())),
                           preferred_element_type=F32)


def _split3(x):
    hi = x.astype(BF16)
    r1 = x - hi.astype(F32)
    mid = r1.astype(BF16)
    lo = (r1 - mid.astype(F32)).astype(BF16)
    return hi, mid, lo


def _dot_sel(x, sel):
    dot = functools.partial(jnp.dot, preferred_element_type=F32)
    hi, mid, lo = _split3(x)
    return dot(hi, sel) + dot(mid, sel) + dot(lo, sel)


def _dot_sel_left(sel, x):
    dot = functools.partial(jnp.dot, preferred_element_type=F32)
    hi, mid, lo = _split3(x)
    return dot(sel, hi) + dot(sel, mid) + dot(sel, lo)


def _sigmoid(x):
    return 1.0 / (1.0 + jnp.exp(-x))


def _silu(x):
    return x * _sigmoid(x)


def _norm_modulate(x, nw, shift, scale):
    y = x * lax.rsqrt(jnp.mean(x * x, axis=-1, keepdims=True) + EPS)
    return (y * nw) * (1.0 + scale) + shift


def _pick(n, options):
    for o in options:
        if n % o == 0:
            return o
    raise ValueError(f"no tile in {options} divides {n}")


def _mod_kernel(c_ref, w_ref, b_ref, o_ref):
    o_ref[...] = _bdot(_silu(c_ref[...]), w_ref[...]) + b_ref[...]


def _modulation(c_rows, w_mod, b_mod):
    nl, d, n6 = w_mod.shape
    rows = c_rows.shape[0]
    tn = _pick(n6, (1536, 1024, 512, 256, 128))
    return pl.pallas_call(
        _mod_kernel,
        grid=(nl, n6 // tn),
        in_specs=[pl.BlockSpec((rows, d), lambda l, j: (0, 0)),
                  pl.BlockSpec((None, d, tn), lambda l, j: (l, 0, j)),
                  pl.BlockSpec((None, 1, tn), lambda l, j: (l, 0, j))],
        out_specs=pl.BlockSpec((None, rows, tn), lambda l, j: (l, 0, j)),
        out_shape=jax.ShapeDtypeStruct((nl, rows, n6), F32),
        compiler_params=_cparams(("parallel", "parallel")),
        name="adaln_modulation",
    )(c_rows, w_mod, b_mod.reshape(nl, 1, n6))


def _inproj_kernel(x_ref, mod_ref, nw_ref, w_ref, *out_refs, d):
    h = _norm_modulate(x_ref[...], nw_ref[...], mod_ref[:, 0:d], mod_ref[:, d:2 * d]).astype(BF16)
    for (a, b), o_ref in zip(SEGS, out_refs):
        o_ref[...] = jnp.dot(h, w_ref[:, a:b], preferred_element_type=F32)


def _in_projection(x, mod, nw, w_pad):
    b, t, d = x.shape
    tm = _pick(t, (512, 256, 128))
    return pl.pallas_call(
        functools.partial(_inproj_kernel, d=d),
        grid=(b, t // tm),
        in_specs=[pl.BlockSpec((None, tm, d), lambda i, j: (i, j, 0)),
                  pl.BlockSpec((None, 1, 6 * d), lambda i, j: (i, 0, 0)),
                  pl.BlockSpec((1, d), lambda i, j: (0, 0)),
                  pl.BlockSpec((d, IN_PAD), lambda i, j: (0, 0))],
        out_specs=[pl.BlockSpec((None, tm, s1 - s0), lambda i, j: (i, j, 0)) for s0, s1 in SEGS],
        out_shape=[jax.ShapeDtypeStruct((b, t, s1 - s0), F32) for s0, s1 in SEGS],
        compiler_params=_cparams(("parallel", "parallel")),
        name="norm_in_projection",
    )(x, mod, nw, w_pad)


def _dn_prep_kernel(xp_ref, x_ref, xn_ref, ab_ref, cw_ref, gp_ref, head_ones_ref, cum_f_ref, cum_b_ref,
                    expand_ref, q_ref, k_ref, v_ref, bf_ref, gf_ref, bb_ref, gb_ref, gc_ref, xe_ref, *, tm):
    i = pl.program_id(1)
    halo = SUBLANES
    pad = (DN_CONV_W - 1) // 2
    zeros = jnp.zeros((halo, 3 * DN_DIM), F32)
    xe_ref[0:halo, :] = jnp.where(i > 0, xp_ref[...], zeros)
    xe_ref[halo:halo + tm, :] = x_ref[...]
    xe_ref[halo + tm:halo + tm + halo, :] = jnp.where(i < pl.num_programs(1) - 1, xn_ref[...], zeros)
    acc = cw_ref[0:1, :] * xe_ref[halo - pad:halo - pad + tm, :]
    for j in range(1, DN_CONV_W):
        acc = acc + cw_ref[j:j + 1, :] * xe_ref[halo - pad + j:halo - pad + j + tm, :]
    qkv = _silu(acc)
    ones = head_ones_ref[...]
    q = qkv[:, 0:DN_DIM]
    k = qkv[:, DN_DIM:2 * DN_DIM]
    q_ref[...] = q * lax.rsqrt(_dot_sel(q * q, ones) + EPS) * (HEAD_DIM ** -0.5)
    k_ref[...] = k * lax.rsqrt(_dot_sel(k * k, ones) + EPS)
    v_ref[...] = qkv[:, 2 * DN_DIM:3 * DN_DIM]

    ab = ab_ref[...]
    lane = lax.broadcasted_iota(jnp.int32, ab.shape, 1)
    z = ab + gp_ref[1:2, :]
    softplus = jnp.maximum(z, 0.0) + jnp.log(1.0 + jnp.exp(-jnp.abs(z)))
    g = -jnp.exp(gp_ref[0:1, :]) * softplus
    g = jnp.where(lane < 2 * DN_HEADS, g, 0.0)
    gc = jnp.where(lane < DN_HEADS, _dot_sel_left(cum_f_ref[...], g), _dot_sel_left(cum_b_ref[...], g))
    gc_ref[...] = gc
    pieces = _split3(jnp.where(lane < 2 * DN_HEADS, gc, _sigmoid(ab)))
    for n, o_ref in enumerate((gf_ref, gb_ref, bf_ref, bb_ref)):
        e = expand_ref[n]
        o_ref[...] = sum(jnp.dot(p, e, preferred_element_type=F32) for p in pieces)


def _head_ones(width):
    idx = jnp.arange(width) // HEAD_DIM
    return (idx[:, None] == idx[None, :]).astype(BF16)


def _dn_prepare(qkv, ab, conv_w, a_log, dt_bias):
    b, t, c = qkv.shape
    tm = _pick(t, (256, 128, 64))
    nt = t // tm
    hb = tm // SUBLANES
    last8 = t // SUBLANES - 1
    gp = jnp.zeros((SUBLANES, LANES), F32)
    gp = gp.at[0, :2 * DN_HEADS].set(a_log.reshape(-1)).at[1, :2 * DN_HEADS].set(dt_bias.reshape(-1))
    r = jnp.arange(tm)
    same = (r[:, None] // DN_CHUNK) == (r[None, :] // DN_CHUNK)
    cum_f = (same & (r[None, :] <= r[:, None])).astype(BF16)
    cum_b = (same & (r[None, :] >= r[:, None])).astype(BF16)
    col_head = jnp.arange(DN_DIM) // HEAD_DIM
    src = jnp.arange(LANES)[:, None]
    expand = jnp.stack([(src == col_head[None, :] + off) for off in
                        (0, DN_HEADS, 2 * DN_HEADS, 3 * DN_HEADS)]).astype(BF16)
    wide = pl.BlockSpec((None, tm, DN_DIM), lambda i, j: (i, j, 0))
    const = lambda shape: pl.BlockSpec(shape, lambda i, j: (0,) * len(shape))
    outs = pl.pallas_call(
        functools.partial(_dn_prep_kernel, tm=tm),
        grid=(b, nt),
        in_specs=[pl.BlockSpec((None, SUBLANES, c), lambda i, j: (i, jnp.maximum(j * hb - 1, 0), 0)),
                  pl.BlockSpec((None, tm, c), lambda i, j: (i, j, 0)),
                  pl.BlockSpec((None, SUBLANES, c), lambda i, j: (i, jnp.minimum((j + 1) * hb, last8), 0)),
                  pl.BlockSpec((None, tm, LANES), lambda i, j: (i, j, 0)),
                  const((DN_CONV_W, c)), const((SUBLANES, LANES)), const((DN_DIM, DN_DIM)),
                  const((tm, tm)), const((tm, tm)), const((4, LANES, DN_DIM))],
        out_specs=[wide] * 7 + [pl.BlockSpec((None, tm, LANES), lambda i, j: (i, j, 0))],
        out_shape=[jax.ShapeDtypeStruct((b, t, DN_DIM), F32)] * 7 + [jax.ShapeDtypeStruct((b, t, LANES), F32)],
        scratch_shapes=[pltpu.VMEM((tm + 2 * SUBLANES, c), F32)],
        compiler_params=_cparams(("parallel", "parallel")),
        name="deltanet_prepare",
    )(qkv, qkv, qkv, ab, conv_w, gp, _head_ones(DN_DIM), cum_f, cum_b, expand)
    return outs


def _tile4(x):
    return jnp.concatenate([x] * HEADS_PER_GROUP, axis=0)


def _group_masks():
    n = HEADS_PER_GROUP * DN_CHUNK
    row = lax.broadcasted_iota(jnp.int32, (n, n), 0)
    col = lax.broadcasted_iota(jnp.int32, (n, n), 1)
    return row, col, (row // DN_CHUNK) == (col // DN_CHUNK)


def _collapse(full, same_head):
    kept = jnp.where(same_head, full, 0.0)
    out = kept[0:DN_CHUNK]
    for h in range(1, HEADS_PER_GROUP):
        out = out + kept[h * DN_CHUNK:(h + 1) * DN_CHUNK]
    return out


def _dn_local_kernel(q_ref, k_ref, v_ref, btf_ref, gxf_ref, btb_ref, gxb_ref, gc_ref, gr_ref,
                     uf_ref, wf_ref, af_ref, ub_ref, wb_ref, ab_ref, *, chunks):
    c = DN_CHUNK
    n = HEADS_PER_GROUP * c
    row, col, same_head = _group_masks()
    ri, cj = row % c, col % c
    eye = (row == col).astype(F32)
    tri = {False: (same_head & (ri >= cj), same_head & (ri > cj)),
           True: (same_head & (ri <= cj), same_head & (ri < cj))}

    def joins(s):
        return ((row // (2 * s)) == (col // (2 * s))) & ((row // s) != (col // s))

    chains = []
    for cc in range(chunks):
        rs = slice(cc * c, (cc + 1) * c)
        for d, (bt_ref, gx_ref, outs) in enumerate(((btf_ref, gxf_ref, (uf_ref, wf_ref, af_ref)),
                                                    (btb_ref, gxb_ref, (ub_ref, wb_ref, ab_ref)))):
            for g in range(DN_GROUPS):
                chains.append((rs, slice(g * MXU_DIM, (g + 1) * MXU_DIM), d, g, cc, bt_ref, gx_ref, outs))

    a_mats, rhs_list = [], []
    for rs, sl, d, g, cc, bt_ref, gx_ref, outs in chains:
        incl, strict = tri[d == 1]
        q, k, v = q_ref[rs, sl], k_ref[rs, sl], v_ref[rs, sl]
        beta, gcx = bt_ref[rs, sl], gx_ref[rs, sl]
        lane0 = d * DN_HEADS + g * HEADS_PER_GROUP
        col_b = jnp.concatenate([jnp.broadcast_to(gc_ref[rs, lane0 + h:lane0 + h + 1], (c, n))
                                 for h in range(HEADS_PER_GROUP)], axis=0)
        diff = col_b - jnp.broadcast_to(gr_ref[d, cc, :, sl], (n, n))
        decay = jnp.where(incl, jnp.exp(jnp.where(incl, diff, 0.0)), 0.0)
        k_beta = k * beta
        k_t = _tile4(k)
        a_mats.append(jnp.where(strict, _bdot_nt(jnp.where(same_head, _tile4(k_beta), 0.0), k_t) * decay, 0.0))
        attn = _bdot_nt(jnp.where(same_head, _tile4(q), 0.0), k_t) * decay
        outs[2][rs, sl] = _collapse(attn, same_head)
        rhs_list.append(jnp.concatenate([_tile4(v * beta), _tile4(k_beta * jnp.exp(gcx))], axis=1))

    invs = [eye - jnp.where(joins(1), a, 0.0) for a in a_mats]
    s = 2
    while s < c:
        js = joins(s)
        mids = [_bdot(jnp.where(js, a, 0.0), inv) for a, inv in zip(a_mats, invs)]
        invs = [inv - _bdot(inv, mid) for inv, mid in zip(invs, mids)]
        s *= 2

    for (rs, sl, d, g, cc, bt_ref, gx_ref, outs), inv, rhs in zip(chains, invs, rhs_list):
        sol = _bdot(inv, rhs)
        outs[0][rs, sl] = _collapse(sol[:, 0:n], same_head)
        outs[1][rs, sl] = _collapse(sol[:, n:2 * n], same_head)


def _dn_local(prep):
    q, k, v, beta_f, gcx_f, beta_b, gcx_b, gc = prep
    b, t, _ = q.shape
    n = t // DN_CHUNK
    chunks = 2 if n % 2 == 0 else 1
    rows = chunks * DN_CHUNK
    g_rows = gc[:, :, :2 * DN_HEADS].reshape(b, n, DN_CHUNK, 2, DN_HEADS)
    g_rows = g_rows.transpose(0, 3, 1, 4, 2).reshape(b, 2, n, 1, DN_DIM)
    wide = pl.BlockSpec((None, rows, DN_DIM), lambda i, j: (i, j, 0))
    return pl.pallas_call(
        functools.partial(_dn_local_kernel, chunks=chunks),
        grid=(b, n // chunks),
        in_specs=[wide] * 7 + [pl.BlockSpec((None, rows, LANES), lambda i, j: (i, j, 0)),
                               pl.BlockSpec((None, 2, chunks, 1, DN_DIM), lambda i, j: (i, 0, j, 0, 0))],
        out_specs=[wide] * 6,
        out_shape=[jax.ShapeDtypeStruct((b, t, DN_DIM), F32)] * 6,
        compiler_params=_cparams(("parallel", "parallel")),
        name="deltanet_chunk_local",
    )(q, k, v, beta_f, gcx_f, beta_b, gcx_b, gc, g_rows)


def _dn_scan_kernel(qf, kf, gxf, uf, wf, af, qb, kb, gxb, ub, wb, ab, s0_ref, of_ref, ob_ref, sfin_ref, s_ref,
                    *, chunks):
    step = pl.program_id(1)
    c = DN_CHUNK

    @pl.when(step == 0)
    def _():
        s_ref[...] = s0_ref[...]

    _, _, same_head = _group_masks()
    dirs = ((qf, kf, gxf, uf, wf, af, of_ref), (qb, kb, gxb, ub, wb, ab, ob_ref))
    states = {(d, g): s_ref[d, g] for d in range(2) for g in range(DN_GROUPS)}
    for cc in range(chunks):
        for d, (q, k, gx, u, w, at, o_ref) in enumerate(dirs):
            reverse = d == 1
            ck = chunks - 1 - cc if reverse else cc
            rs = slice(ck * c, (ck + 1) * c)
            for g in range(DN_GROUPS):
                sl = slice(g * MXU_DIM, (g + 1) * MXU_DIM)
                state = states[(d, g)]
                gcx = gx[rs, sl]
                g_last = gcx[0:1, :] if reverse else gcx[c - 1:c, :]
                q_dec = q[rs, sl] * jnp.exp(gcx)
                k_dec = k[rs, sl] * jnp.exp(g_last - gcx)
                ws = _bdot(jnp.concatenate([w[rs, sl], q_dec], axis=0), state)
                v_new = u[rs, sl] - ws[0:c]
                attn = jnp.where(same_head, _tile4(at[rs, sl]), 0.0)
                o_ref[rs, sl] = ws[c:2 * c] + _collapse(_bdot(attn, _tile4(v_new)), same_head)
                states[(d, g)] = state * jnp.exp(g_last) + jnp.where(same_head, _bdot_tn(k_dec, v_new), 0.0)
    for (d, g), state in states.items():
        s_ref[d, g] = state

    @pl.when(step == pl.num_programs(1) - 1)
    def _():
        sfin_ref[...] = s_ref[...]


def _dn_scan(prep, s0):
    q, k, _, _, gcx_f, _, gcx_b, _ = prep
    u_f, w_f, a_f, u_b, w_b, a_b = _dn_local(prep)
    b, t, _ = q.shape
    n = t // DN_CHUNK
    chunks = 2 if n % 2 == 0 else 1
    rows = chunks * DN_CHUNK
    steps = n // chunks
    fwd = pl.BlockSpec((None, rows, DN_DIM), lambda i, j: (i, j, 0))
    bwd = pl.BlockSpec((None, rows, DN_DIM), lambda i, j: (i, steps - 1 - j, 0))
    state = pl.BlockSpec((None, 2, DN_GROUPS, MXU_DIM, MXU_DIM), lambda i, j: (i, 0, 0, 0, 0))
    o_f, o_b, s_fin = pl.pallas_call(
        functools.partial(_dn_scan_kernel, chunks=chunks),
        grid=(b, steps),
        in_specs=[fwd] * 6 + [bwd] * 6 + [state],
        out_specs=[fwd, bwd, state],
        out_shape=[jax.ShapeDtypeStruct((b, t, DN_DIM), F32)] * 2
        + [jax.ShapeDtypeStruct((b, 2, DN_GROUPS, MXU_DIM, MXU_DIM), F32)],
        scratch_shapes=[pltpu.VMEM((2, DN_GROUPS, MXU_DIM, MXU_DIM), F32)],
        compiler_params=_cparams(("parallel", "arbitrary")),
        name="deltanet_scan",
    )(q, k, gcx_f, u_f, w_f, a_f, q, k, gcx_b, u_b, w_b, a_b, s0)
    return o_f, o_b, s_fin


def _rope(x, cos_t, sin_t):
    width = x.shape[-1]
    half = HEAD_DIM // 2
    lane = lax.broadcasted_iota(jnp.int32, x.shape, 1)
    partner = jnp.where(lane % HEAD_DIM < half, pltpu.roll(x, width - half, 1), pltpu.roll(x, half, 1))
    reps = width // LANES
    cos_w = jnp.concatenate([cos_t] * reps, axis=1) if reps > 1 else cos_t
    sin_w = jnp.concatenate([sin_t] * reps, axis=1) if reps > 1 else sin_t
    return x * cos_w + partner * sin_w


def _head_rmsnorm(x, ones, w):
    ms = _dot_sel(x * x, ones) * (1.0 / HEAD_DIM)
    return x * lax.rsqrt(ms + EPS) * w


def _attn_prep_kernel(q_ref, k_ref, v_ref, qw_ref, kw_ref, cos_ref, sin_ref, ones_q_ref, ones_k_ref,
                      qo_ref, ko_ref, vo_ref, *, rope):
    q = _head_rmsnorm(q_ref[...], ones_q_ref[...], qw_ref[...])
    k = _head_rmsnorm(k_ref[...], ones_k_ref[...], kw_ref[...])
    if rope:
        q = _rope(q, cos_ref[...], sin_ref[...])
        k = _rope(k, cos_ref[...], sin_ref[...])
    qo_ref[...] = (q * (HEAD_DIM ** -0.5 * LOG2E)).astype(BF16)
    v = v_ref[...]
    lane = lax.broadcasted_iota(jnp.int32, k.shape, 1)
    first = lane < HEAD_DIM
    k_sw = pltpu.roll(k, HEAD_DIM, 1)
    v_sw = pltpu.roll(v, HEAD_DIM, 1)
    ko_ref[0] = jnp.where(first, k, k_sw).astype(BF16)
    ko_ref[1] = jnp.where(first, k_sw, k).astype(BF16)
    vo_ref[0] = jnp.where(first, v, 1.0).astype(BF16)
    vo_ref[1] = jnp.where(first, v_sw, 1.0).astype(BF16)


def _rope_tables(t):
    rows = t // GRID_W
    row_pos = jnp.repeat(jnp.arange(rows, dtype=F32), GRID_W, total_repeat_length=t)
    col_pos = jnp.tile(jnp.arange(GRID_W, dtype=F32), rows)
    n_freq = HEAD_DIM // 4
    freqs = ROPE_THETA ** (-jnp.arange(n_freq, dtype=F32) / n_freq)
    ang = jnp.concatenate([row_pos[:, None] * freqs, col_pos[:, None] * freqs], axis=-1)
    cos, sin = jnp.cos(ang), jnp.sin(ang)
    reps = LANES // HEAD_DIM
    return (jnp.tile(jnp.concatenate([cos, cos], axis=-1), (1, reps)),
            jnp.tile(jnp.concatenate([-sin, sin], axis=-1), (1, reps)))


def _attn_prepare(atq, atk, atv, qn_w, kn_w, rope):
    b, t, _ = atq.shape
    tm = _pick(t, (512, 256, 128))
    if rope:
        cos_t, sin_t = _rope_tables(t)
    else:
        cos_t = sin_t = jnp.zeros((t, LANES), F32)
    qw = jnp.tile(qn_w, AT_Q_HEADS).reshape(1, AT_Q_DIM)
    kw = jnp.tile(kn_w, AT_KV_HEADS).reshape(1, AT_KV_DIM)
    const = lambda shape: pl.BlockSpec(shape, lambda i, j: (0,) * len(shape))
    tab = pl.BlockSpec((tm, LANES), lambda i, j: (j, 0))
    kv_out = pl.BlockSpec((None, AT_KV_HEADS, tm, AT_KV_DIM), lambda i, j: (i, 0, j, 0))
    return pl.pallas_call(
        functools.partial(_attn_prep_kernel, rope=rope),
        grid=(b, t // tm),
        in_specs=[pl.BlockSpec((None, tm, AT_Q_DIM), lambda i, j: (i, j, 0)),
                  pl.BlockSpec((None, tm, AT_KV_DIM), lambda i, j: (i, j, 0)),
                  pl.BlockSpec((None, tm, AT_KV_DIM), lambda i, j: (i, j, 0)),
                  const((1, AT_Q_DIM)), const((1, AT_KV_DIM)), tab, tab,
                  const((AT_Q_DIM, AT_Q_DIM)), const((AT_KV_DIM, AT_KV_DIM))],
        out_specs=[pl.BlockSpec((None, tm, AT_Q_DIM), lambda i, j: (i, j, 0)), kv_out, kv_out],
        out_shape=[jax.ShapeDtypeStruct((b, t, AT_Q_DIM), BF16),
                   jax.ShapeDtypeStruct((b, AT_KV_HEADS, t, AT_KV_DIM), BF16),
                   jax.ShapeDtypeStruct((b, AT_KV_HEADS, t, AT_KV_DIM), BF16)],
        compiler_params=_cparams(("parallel", "parallel")),
        name="attention_prepare",
    )(atq, atk, atv, qw, kw, cos_t, sin_t, _head_ones(AT_Q_DIM), _head_ones(AT_KV_DIM))


def _flash_kernel(q_ref, k_ref, v_ref, o_ref, qs_ref, m_ref, acc_ref, *, tq):
    j = pl.program_id(3)

    @pl.when(j == 0)
    def _():
        q = q_ref[...]
        lane = lax.broadcasted_iota(jnp.int32, q.shape, 1)
        for h in range(AT_GROUP):
            qs_ref[h * tq:(h + 1) * tq, :] = jnp.where(lane // HEAD_DIM == h, q, jnp.zeros_like(q))
        m_ref[...] = jnp.full(m_ref.shape, -jnp.inf, F32)
        acc_ref[...] = jnp.zeros(acc_ref.shape, F32)

    k = k_ref[...]
    k2 = jnp.concatenate([k, k], axis=1)
    v = v_ref[...]
    blocks = [slice(r0, r0 + FLASH_ROWS) for r0 in range(0, AT_GROUP * tq, FLASH_ROWS)]
    scores = [lax.dot_general(qs_ref[rows, :], k2, (((1,), (1,)), ((), ())), preferred_element_type=F32)
              for rows in blocks]
    m_prev = [m_ref[rows, :] for rows in blocks]
    acc_prev = [acc_ref[rows, :] for rows in blocks]
    m_next = [jnp.maximum(mp, jnp.max(s, axis=1, keepdims=True)) for mp, s in zip(m_prev, scores)]
    probs = [jnp.exp2(s - mn[:, 0:1]).astype(BF16) for s, mn in zip(scores, m_next)]
    pv = [jnp.dot(p, v, preferred_element_type=F32) for p in probs]
    for rows, mp, mn, ap, x in zip(blocks, m_prev, m_next, acc_prev, pv):
        acc_ref[rows, :] = jnp.exp2(mp - mn) * ap + x
        m_ref[rows, :] = mn

    @pl.when(j == pl.num_programs(3) - 1)
    def _():
        lane = lax.broadcasted_iota(jnp.int32, (tq, LANES), 1)
        first = lane < HEAD_DIM

        def normalised(h):
            a = acc_ref[h * tq:(h + 1) * tq, :]
            return a * pltpu.roll(1.0 / a, HEAD_DIM, 1)

        pairs = [jnp.where(first, normalised(2 * p2), pltpu.roll(normalised(2 * p2 + 1), HEAD_DIM, 1))
                 for p2 in range(AT_GROUP // 2)]
        o_ref[...] = jnp.concatenate(pairs, axis=1)


def _flash_attention(q, kd, vd):
    b, t, _ = q.shape
    s = kd.shape[2]
    tq = _pick(t, (512, 256, 128))
    tk = _pick(s, (768, 512, 256, 128))
    gw = AT_GROUP * HEAD_DIM
    return pl.pallas_call(
        functools.partial(_flash_kernel, tq=tq),
        grid=(b, AT_KV_HEADS, t // tq, s // tk),
        in_specs=[pl.BlockSpec((None, tq, gw), lambda i, g, a, j: (i, a, g)),
                  pl.BlockSpec((None, None, tk, AT_KV_DIM), lambda i, g, a, j: (i, g, j, 0)),
                  pl.BlockSpec((None, None, tk, AT_KV_DIM), lambda i, g, a, j: (i, g, j, 0))],
        out_specs=pl.BlockSpec((None, tq, gw), lambda i, g, a, j: (i, a, g)),
        out_shape=jax.ShapeDtypeStruct((b, t, AT_Q_DIM), F32),
        scratch_shapes=[pltpu.VMEM((AT_GROUP * tq, gw), BF16),
                        pltpu.VMEM((AT_GROUP * tq, LANES), F32),
                        pltpu.VMEM((AT_GROUP * tq, LANES), F32)],
        compiler_params=_cparams(("parallel", "parallel", "parallel", "arbitrary")),
        name="gqa_flash_attention",
    )(q, kd, vd)


def _outproj_kernel(of_ref, ob_ref, z_ref, at_ref, x_ref, mod_ref, nw_ref, ones_ref, w_ref, o_ref, *, d):
    o = of_ref[...] + ob_ref[...]
    ms = _dot_sel(o * o, ones_ref[...]) * (1.0 / HEAD_DIM)
    dn = o * lax.rsqrt(ms + EPS) * nw_ref[...] * _silu(z_ref[...])
    mix = _bdot(dn, w_ref[0:DN_DIM, :]) + _bdot(at_ref[...], w_ref[DN_DIM:DN_DIM + AT_Q_DIM, :])
    o_ref[...] = x_ref[...] + mod_ref[:, 2 * d:3 * d] * mix


def _out_projection(o_f, o_b, z, at, x, mod, dn_norm_w, w_out):
    b, t, d = x.shape
    tm = _pick(t, (512, 256, 128))
    half = pl.BlockSpec((None, tm, DN_DIM), lambda i, j: (i, j, 0))
    full = pl.BlockSpec((None, tm, d), lambda i, j: (i, j, 0))
    const = lambda shape: pl.BlockSpec(shape, lambda i, j: (0,) * len(shape))
    return pl.pallas_call(
        functools.partial(_outproj_kernel, d=d),
        grid=(b, t // tm),
        in_specs=[half, half, half, half, full, pl.BlockSpec((None, 1, 6 * d), lambda i, j: (i, 0, 0)),
                  const((1, DN_DIM)), const((DN_DIM, DN_DIM)), const((DN_DIM + AT_Q_DIM, d))],
        out_specs=full,
        out_shape=jax.ShapeDtypeStruct((b, t, d), F32),
        compiler_params=_cparams(("parallel", "parallel")),
        name="mix_out_projection",
    )(o_f, o_b, z, at, x, mod, jnp.tile(dn_norm_w, DN_HEADS).reshape(1, DN_DIM), _head_ones(DN_DIM), w_out)


def _ffn_kernel(x_ref, mod_ref, nw_ref, fw_ref, wg_ref, wu_ref, wd_ref, o_ref, h_ref, acc_ref, *, d, final):
    f = pl.program_id(2)

    @pl.when(f == 0)
    def _():
        h_ref[...] = _norm_modulate(x_ref[...], nw_ref[...], mod_ref[:, 3 * d:4 * d],
                                    mod_ref[:, 4 * d:5 * d]).astype(BF16)
        acc_ref[...] = jnp.zeros(acc_ref.shape, F32)

    h = h_ref[...]
    gate = jnp.dot(h, wg_ref[...], preferred_element_type=F32)
    up = jnp.dot(h, wu_ref[...], preferred_element_type=F32)
    acc_ref[...] += jnp.dot((_silu(gate) * up).astype(BF16), wd_ref[...], preferred_element_type=F32)

    @pl.when(f == pl.num_programs(2) - 1)
    def _():
        y = x_ref[...] + mod_ref[:, 5 * d:6 * d] * acc_ref[...]
        if final:
            y = y * lax.rsqrt(jnp.mean(y * y, axis=-1, keepdims=True) + EPS) * fw_ref[...]
        o_ref[...] = y


def _dense_ffn(x, mod, nw, final_w, wg, wu, wd, final):
    b, t, d = x.shape
    fdim = wg.shape[1]
    tm = _pick(t, (1024, 512, 256, 128))
    tf = _pick(fdim, (512, 256, 128))
    full = pl.BlockSpec((None, tm, d), lambda i, j, f: (i, j, 0))
    const = lambda shape: pl.BlockSpec(shape, lambda i, j, f: (0,) * len(shape))
    return pl.pallas_call(
        functools.partial(_ffn_kernel, d=d, final=final),
        grid=(b, t // tm, fdim // tf),
        in_specs=[full, pl.BlockSpec((None, 1, 6 * d), lambda i, j, f: (i, 0, 0)), const((1, d)), const((1, d)),
                  pl.BlockSpec((d, tf), lambda i, j, f: (0, f)),
                  pl.BlockSpec((d, tf), lambda i, j, f: (0, f)),
                  pl.BlockSpec((tf, d), lambda i, j, f: (f, 0))],
        out_specs=full,
        out_shape=jax.ShapeDtypeStruct((b, t, d), F32),
        scratch_shapes=[pltpu.VMEM((tm, d), BF16), pltpu.VMEM((tm, d), F32)],
        compiler_params=_cparams(("parallel", "parallel", "arbitrary")),
        name="dense_swiglu",
    )(x, mod, nw, final_w, wg, wu, wd)


R_E1, R_E2, R_W1, R_W2, R_RANK1, R_RANK2 = range(6)


def _router_kernel(x_ref, mod_ref, nw_ref, rw_ref, rb_ref, tri_ref, h_ref, route_ref, cnt_ref, carry_ref, *, d):
    @pl.when(pl.program_id(0) == 0)
    def _():
        carry_ref[...] = jnp.zeros(carry_ref.shape, F32)

    h = _norm_modulate(x_ref[...], nw_ref[...], mod_ref[:, 3 * d:4 * d], mod_ref[:, 4 * d:5 * d])
    h_ref[...] = h
    logits = _bdot(h, rw_ref[...]) + rb_ref[...]
    lane = lax.broadcasted_iota(jnp.int32, logits.shape, 1).astype(F32)
    neg = jnp.float32(-jnp.inf)
    lg = jnp.where(lane < N_EXPERTS, logits, neg)
    m1 = jnp.max(lg, axis=1, keepdims=True)
    i1 = jnp.min(jnp.where(lg == m1, lane, float(LANES)), axis=1, keepdims=True)
    hot1 = lane == i1
    lg2 = jnp.where(hot1, neg, lg)
    m2 = jnp.max(lg2, axis=1, keepdims=True)
    i2 = jnp.min(jnp.where(lg2 == m2, lane, float(LANES)), axis=1, keepdims=True)
    hot2 = lane == i2
    e2 = jnp.exp(m2 - m1)
    w1 = 1.0 / (1.0 + e2)
    w2 = e2 / (1.0 + e2)
    member = jnp.where(hot1 | hot2, 1.0, 0.0)
    before = jnp.dot(tri_ref[...], member.astype(BF16), preferred_element_type=F32) + carry_ref[0:1, :]
    rank1 = jnp.sum(jnp.where(hot1, before, 0.0), axis=1, keepdims=True)
    rank2 = jnp.sum(jnp.where(hot2, before, 0.0), axis=1, keepdims=True)
    rec = jnp.zeros(logits.shape, F32)
    for ln, val in ((R_E1, i1), (R_E2, i2), (R_W1, w1), (R_W2, w2), (R_RANK1, rank1), (R_RANK2, rank2)):
        rec = jnp.where(lane == float(ln), val, rec)
    route_ref[...] = rec
    tb = member.shape[0]
    total = before[tb - 1:tb, :] + member[tb - 1:tb, :]
    carry_ref[...] = jnp.broadcast_to(total, carry_ref.shape)
    cnt_ref[...] = jnp.broadcast_to(total, cnt_ref.shape)


def _route(x2, mod, nw, router_w, router_b, rows_per_mod):
    n, d = x2.shape
    tb = _pick(rows_per_mod, (512, 256, 128))
    per = rows_per_mod // tb
    rw = jnp.zeros((d, LANES), BF16).at[:, :N_EXPERTS].set(router_w.astype(BF16))
    rb = jnp.zeros((1, LANES), F32).at[0, :N_EXPERTS].set(router_b)
    r = jnp.arange(tb)
    tri = (r[None, :] < r[:, None]).astype(BF16)
    const = lambda shape: pl.BlockSpec(shape, lambda i: (0,) * len(shape))
    return pl.pallas_call(
        functools.partial(_router_kernel, d=d),
        grid=(n // tb,),
        in_specs=[pl.BlockSpec((tb, d), lambda i: (i, 0)),
                  pl.BlockSpec((None, 1, 6 * d), lambda i: (i // per, 0, 0)),
                  const((1, d)), const((d, LANES)), const((1, LANES)), const((tb, tb))],
        out_specs=[pl.BlockSpec((tb, d), lambda i: (i, 0)), pl.BlockSpec((tb, LANES), lambda i: (i, 0)),
                   const((SUBLANES, LANES))],
        out_shape=[jax.ShapeDtypeStruct((n, d), F32), jax.ShapeDtypeStruct((n, LANES), F32),
                   jax.ShapeDtypeStruct((SUBLANES, LANES), F32)],
        scratch_shapes=[pltpu.VMEM((SUBLANES, LANES), F32)],
        compiler_params=_cparams(("arbitrary",)),
        name="moe_router_top2",
    )(x2, mod, nw, rw, rb, tri)


def _dispatch_kernel(dest_ref, h_ref, xs_in_ref, xs_ref, sem, *, tb):
    del xs_in_ref

    def copy(t, kk):
        return pltpu.make_async_copy(h_ref.at[pl.ds(t, 1), :],
                                     xs_ref.at[pl.ds(dest_ref[0, 2 * t + kk], 1), :], sem)

    def start(t, c):
        copy(t, 0).start(priority=0)
        copy(t, 1).start(priority=1)
        return c

    def wait(t, c):
        copy(t, 0).wait()
        copy(t, 1).wait()
        return c

    lax.fori_loop(0, tb, start, 0)
    lax.fori_loop(0, tb, wait, 0)


def _dispatch(h, dest, p_rows):
    n, d = h.shape
    tb = _pick(n, (256, 128))
    return pl.pallas_call(
        functools.partial(_dispatch_kernel, tb=tb),
        grid=(n // tb,),
        in_specs=[pl.BlockSpec((None, 1, 2 * tb), lambda i: (i, 0, 0), memory_space=pltpu.SMEM),
                  pl.BlockSpec((tb, d), lambda i: (i, 0)), pl.BlockSpec(memory_space=pl.ANY)],
        out_specs=pl.BlockSpec(memory_space=pl.ANY),
        out_shape=jax.ShapeDtypeStruct((p_rows, d), F32),
        scratch_shapes=[pltpu.SemaphoreType.DMA(())],
        input_output_aliases={2: 0},
        compiler_params=_cparams(("arbitrary",)),
        name="moe_dispatch_rows",
    )(dest.reshape(n // tb, 1, 2 * tb), h, jnp.zeros((p_rows, d), F32))


def _expert_kernel(be_ref, bx_ref, bv_ref, x_ref, wg_ref, wu_ref, wd_ref, o_ref, h_ref, acc_ref):
    i, f = pl.program_id(0), pl.program_id(1)
    valid = bv_ref[i] == 1
    last = f == pl.num_programs(1) - 1

    @pl.when(valid & (f == 0))
    def _():
        h_ref[...] = x_ref[...].astype(BF16)
        acc_ref[...] = jnp.zeros(acc_ref.shape, F32)

    @pl.when(valid)
    def _():
        h = h_ref[...]
        gate = jnp.dot(h, wg_ref[...], preferred_element_type=F32)
        up = jnp.dot(h, wu_ref[...], preferred_element_type=F32)
        acc_ref[...] += jnp.dot((_silu(gate) * up).astype(BF16), wd_ref[...], preferred_element_type=F32)

    @pl.when(valid & last)
    def _():
        o_ref[...] = acc_ref[...]

    @pl.when(jnp.logical_not(valid) & last)
    def _():
        o_ref[...] = jnp.zeros(o_ref.shape, F32)


def _experts(xs, blk_e, blk_x, blk_v, wg, wu, wd, mb):
    p_rows, d = xs.shape
    fdim = wg.shape[2]
    tf = _pick(fdim, (512, 256, 128))
    nf = fdim // tf
    fsel = lambda i, f, bv: jnp.where(bv[i] == 1, f, nf - 1)
    grid_spec = pltpu.PrefetchScalarGridSpec(
        num_scalar_prefetch=3,
        grid=(p_rows // mb, nf),
        in_specs=[pl.BlockSpec((mb, d), lambda i, f, be, bx, bv: (bx[i], 0)),
                  pl.BlockSpec((None, d, tf), lambda i, f, be, bx, bv: (be[i], 0, fsel(i, f, bv))),
                  pl.BlockSpec((None, d, tf), lambda i, f, be, bx, bv: (be[i], 0, fsel(i, f, bv))),
                  pl.BlockSpec((None, tf, d), lambda i, f, be, bx, bv: (be[i], fsel(i, f, bv), 0))],
        out_specs=pl.BlockSpec((mb, d), lambda i, f, be, bx, bv: (i, 0)),
        scratch_shapes=[pltpu.VMEM((mb, d), BF16), pltpu.VMEM((mb, d), F32)])
    return pl.pallas_call(
        _expert_kernel,
        grid_spec=grid_spec,
        out_shape=jax.ShapeDtypeStruct((p_rows, d), F32),
        compiler_params=_cparams(("arbitrary", "arbitrary")),
        name="moe_expert_swiglu",
    )(blk_e, blk_x, blk_v, xs, wg, wu, wd)


def _combine_kernel(dest_ref, ys_ref, x_ref, route_ref, mod_ref, fw_ref, o_ref, y1_ref, y2_ref, sem, *,
                    d, tb, final):
    def copy(t, kk, buf):
        return pltpu.make_async_copy(ys_ref.at[pl.ds(dest_ref[0, 2 * t + kk], 1), :],
                                     buf.at[pl.ds(t, 1), :], sem)

    def start(t, c):
        copy(t, 0, y1_ref).start(priority=0)
        copy(t, 1, y2_ref).start(priority=1)
        return c

    def wait(t, c):
        copy(t, 0, y1_ref).wait()
        copy(t, 1, y2_ref).wait()
        return c

    lax.fori_loop(0, tb, start, 0)
    lax.fori_loop(0, tb, wait, 0)
    w1 = route_ref[:, R_W1:R_W1 + 1]
    w2 = route_ref[:, R_W2:R_W2 + 1]
    y = x_ref[...] + mod_ref[:, 5 * d:6 * d] * (y1_ref[...] * w1 + y2_ref[...] * w2)
    if final:
        y = y * lax.rsqrt(jnp.mean(y * y, axis=-1, keepdims=True) + EPS) * fw_ref[...]
    o_ref[...] = y


def _combine(ys, dest, x2, route, mod, final_w, rows_per_mod, final):
    n, d = x2.shape
    tb = _pick(rows_per_mod, (256, 128))
    per = rows_per_mod // tb
    return pl.pallas_call(
        functools.partial(_combine_kernel, d=d, tb=tb, final=final),
        grid=(n // tb,),
        in_specs=[pl.BlockSpec((None, 1, 2 * tb), lambda i: (i, 0, 0), memory_space=pltpu.SMEM),
                  pl.BlockSpec(memory_space=pl.ANY),
                  pl.BlockSpec((tb, d), lambda i: (i, 0)),
                  pl.BlockSpec((tb, LANES), lambda i: (i, 0)),
                  pl.BlockSpec((None, 1, 6 * d), lambda i: (i // per, 0, 0)),
                  pl.BlockSpec((1, d), lambda i: (0, 0))],
        out_specs=pl.BlockSpec((tb, d), lambda i: (i, 0)),
        out_shape=jax.ShapeDtypeStruct((n, d), F32),
        scratch_shapes=[pltpu.VMEM((tb, d), F32), pltpu.VMEM((tb, d), F32), pltpu.SemaphoreType.DMA(())],
        compiler_params=_cparams(("arbitrary",)),
        name="moe_combine_rows",
    )(dest.reshape(n // tb, 1, 2 * tb), ys, x2, route, mod, final_w)


def _moe_ffn(x, mod, nw, final_w, router_w, router_b, wg, wu, wd, final):
    b, t, d = x.shape
    n = b * t
    x2 = x.reshape(n, d)
    h, route, counts = _route(x2, mod, nw, router_w, router_b, t)
    mb = _pick(n, (1024, 512, 256, 128))
    cnt = counts[0, :N_EXPERTS].astype(jnp.int32)
    nblk_e = (cnt + mb - 1) // mb
    blk_end = jnp.cumsum(nblk_e)
    pad_start = (blk_end - nblk_e) * mb
    nblk = (2 * n) // mb + N_EXPERTS
    bi = jnp.arange(nblk, dtype=jnp.int32)
    total_blk = blk_end[-1]
    blk_v = (bi < total_blk).astype(jnp.int32)
    clamped = jnp.minimum(bi, total_blk - 1).astype(jnp.int32)
    blk_e = jnp.sum((clamped[:, None] >= blk_end[None, :]).astype(jnp.int32), axis=1)
    blk_e = jnp.minimum(blk_e, N_EXPERTS - 1).astype(jnp.int32)
    e12 = route[:, R_E1:R_E2 + 1].astype(jnp.int32)
    rank12 = route[:, R_RANK1:R_RANK2 + 1].astype(jnp.int32)
    dest = (pad_start[e12] + rank12).reshape(n * 2)
    xs = _dispatch(h, dest, nblk * mb)
    ys = _experts(xs, blk_e, clamped, blk_v, wg, wu, wd, mb)
    out = _combine(ys, dest, x2, route, mod, final_w, t, final)
    return out.reshape(b, t, d)


def _pad_in_weight(w):
    d = w.shape[0]
    ab_end = 4 * DN_DIM + 4 * DN_HEADS
    return jnp.concatenate([w[:, :ab_end], jnp.zeros((d, SEG_AB[1] - ab_end), w.dtype), w[:, ab_end:]],
                           axis=1).astype(BF16)


def kernel(x, c, ctx, c_ctx, w_mod, b_mod, norm1_w, norm2_w, w_in, dn_conv_w, dn_a_log, dn_dt_bias, dn_norm_w, at_qnorm_w, at_knorm_w, w_out, ffn_w_gate, ffn_w_up, ffn_w_down, moe_router_w, moe_router_b, moe_w_gate, moe_w_up, moe_w_down, final_norm_w):
    b, t, d = x.shape
    depth = w_mod.shape[0]
    xc = ctx
    rows = -(-(b + 1) // SUBLANES) * SUBLANES
    c_rows = jnp.zeros((rows, d), F32).at[:b].set(c).at[b].set(c_ctx)
    mod_all = _modulation(c_rows, w_mod, b_mod)
    final_w = final_norm_w.reshape(1, d)
    zero_state = jnp.zeros((b, 2, DN_GROUPS, MXU_DIM, MXU_DIM), F32)

    for layer in range(depth):
        last = layer == depth - 1
        mod = mod_all[layer, :b].reshape(b, 1, 6 * d)
        mod_c = jnp.broadcast_to(mod_all[layer, b].reshape(1, 1, 6 * d), (b, 1, 6 * d))
        n1 = norm1_w[layer].reshape(1, d)
        n2 = norm2_w[layer].reshape(1, d)
        w_pad = _pad_in_weight(w_in[layer])
        w_o = w_out[layer].astype(BF16)

        qkv, z, ab, atq, atk, atv = _in_projection(x, mod, n1, w_pad)
        qkv_c, z_c, ab_c, atq_c, atk_c, atv_c = _in_projection(xc, mod_c, n1, w_pad)

        prep_c = _dn_prepare(qkv_c, ab_c, dn_conv_w[layer], dn_a_log[layer], dn_dt_bias[layer])
        of_c, ob_c, s_ctx = _dn_scan(prep_c, zero_state)
        prep = _dn_prepare(qkv, ab, dn_conv_w[layer], dn_a_log[layer], dn_dt_bias[layer])
        o_f, o_b, _ = _dn_scan(prep, s_ctx)

        q_l, kd_l, vd_l = _attn_prepare(atq, atk, atv, at_qnorm_w[layer], at_knorm_w[layer], rope=True)
        q_c, kd_c, vd_c = _attn_prepare(atq_c, atk_c, atv_c, at_qnorm_w[layer], at_knorm_w[layer], rope=False)
        at = _flash_attention(q_l, jnp.concatenate([kd_l, kd_c], axis=2), jnp.concatenate([vd_l, vd_c], axis=2))

        x = _out_projection(o_f, o_b, z, at, x, mod, dn_norm_w[layer], w_o)
        if not last:
            at_c = _flash_attention(q_c, kd_c, vd_c)
            xc = _out_projection(of_c, ob_c, z_c, at_c, xc, mod_c, dn_norm_w[layer], w_o)

        if layer % 2 == 0:
            i = layer // 2
            wg, wu, wd = (w[i].astype(BF16) for w in (ffn_w_gate, ffn_w_up, ffn_w_down))
            x = _dense_ffn(x, mod, n2, final_w, wg, wu, wd, last)
            if not last:
                xc = _dense_ffn(xc, mod_c, n2, final_w, wg, wu, wd, False)
        else:
            j = layer // 2
            wg, wu, wd = (w[j].astype(BF16) for w in (moe_w_gate, moe_w_up, moe_w_down))
            x = _moe_ffn(x, mod, n2, final_w, moe_router_w[j], moe_router_b[j], wg, wu, wd, last)
            if not last:
                xc = _moe_ffn(xc, mod_c, n2, final_w, moe_router_w[j], moe_router_b[j], wg, wu, wd, False)
    return x
```

```python
import functools
import math

import jax
import jax.numpy as jnp
from jax import lax
from jax.experimental import pallas as pl
from jax.experimental.pallas import tpu as pltpu

F32 = jnp.float32
BF16 = jnp.bfloat16
EPS = 1e-6
LOG2E = math.log2(math.e)

DN_HEADS = 8
HEAD_DIM = 64
DN_DIM = DN_HEADS * HEAD_DIM
DN_CONV_W = 5
DN_CHUNK = 64
AT_Q_HEADS = 8
AT_KV_HEADS = 2
AT_GROUP = AT_Q_HEADS // AT_KV_HEADS
AT_Q_DIM = AT_Q_HEADS * HEAD_DIM
AT_KV_DIM = AT_KV_HEADS * HEAD_DIM
GRID_W = 64
ROPE_THETA = 10000.0
N_EXPERTS = 8

LANES = 128
SUBLANES = 8
MXU_DIM = 256
VMEM_LIMIT = 52 * 1024 * 1024

HEADS_PER_GROUP = MXU_DIM // HEAD_DIM
DN_GROUPS = DN_HEADS // HEADS_PER_GROUP
FLASH_ROWS = 256
DMA_UNROLL = 8

SEG_QKV = (0, 3 * DN_DIM)
SEG_Z = (3 * DN_DIM, 4 * DN_DIM)
SEG_AB = (4 * DN_DIM, 4 * DN_DIM + LANES)
SEG_ATQ = (SEG_AB[1], SEG_AB[1] + AT_Q_DIM)
SEG_ATK = (SEG_ATQ[1], SEG_ATQ[1] + AT_KV_DIM)
SEG_ATV = (SEG_ATK[1], SEG_ATK[1] + AT_KV_DIM)
IN_PAD = SEG_ATV[1]
SEGS = (SEG_QKV, SEG_Z, SEG_AB, SEG_ATQ, SEG_ATK, SEG_ATV)


def _cparams(sem):
    return pltpu.CompilerParams(dimension_semantics=sem, vmem_limit_bytes=VMEM_LIMIT)


def _bdot(a, b):
    return jnp.dot(a.astype(BF16), b.astype(BF16), preferred_element_type=F32)


def _bdot_nt(a, b):
    return lax.dot_general(a.astype(BF16), b.astype(BF16), (((1,), (1,)), ((), ())),
                           preferred_element_type=F32)


def _bdot_tn(a, b):
    return lax.dot_general(a.astype(BF16), b.astype(BF16), (((0,), (0,)), ((), ())),
                           preferred_element_type=F32)


def _split3(x):
    hi = x.astype(BF16)
    r1 = x - hi.astype(F32)
    mid = r1.astype(BF16)
    lo = (r1 - mid.astype(F32)).astype(BF16)
    return hi, mid, lo


def _dot_sel(x, sel):
    dot = functools.partial(jnp.dot, preferred_element_type=F32)
    hi, mid, lo = _split3(x)
    return dot(hi, sel) + dot(mid, sel) + dot(lo, sel)


def _dot_sel_left(sel, x):
    dot = functools.partial(jnp.dot, preferred_element_type=F32)
    hi, mid, lo = _split3(x)
    return dot(sel, hi) + dot(sel, mid) + dot(sel, lo)


def _sigmoid(x):
    return 1.0 / (1.0 + jnp.exp(-x))


def _silu(x):
    return x * _sigmoid(x)


def _norm_modulate(x, nw, shift, scale):
    y = x * lax.rsqrt(jnp.mean(x * x, axis=-1, keepdims=True) + EPS)
    return (y * nw) * (1.0 + scale) + shift


def _pick(n, options):
    for o in options:
        if n % o == 0:
            return o
    raise ValueError(f"no tile in {options} divides {n}")


def _mod_kernel(c_ref, w_ref, b_ref, o_ref):
    o_ref[...] = _bdot(_silu(c_ref[...]), w_ref[...]) + b_ref[...]


def _modulation(c_rows, w_mod, b_mod):
    nl, d, n6 = w_mod.shape
    rows = c_rows.shape[0]
    tn = _pick(n6, (1536, 1024, 512, 256, 128))
    return pl.pallas_call(
        _mod_kernel,
        grid=(nl, n6 // tn),
        in_specs=[pl.BlockSpec((rows, d), lambda l, j: (0, 0)),
                  pl.BlockSpec((None, d, tn), lambda l, j: (l, 0, j)),
                  pl.BlockSpec((None, 1, tn), lambda l, j: (l, 0, j))],
        out_specs=pl.BlockSpec((None, rows, tn), lambda l, j: (l, 0, j)),
        out_shape=jax.ShapeDtypeStruct((nl, rows, n6), F32),
        compiler_params=_cparams(("parallel", "parallel")),
        name="adaln_modulation",
    )(c_rows, w_mod, b_mod.reshape(nl, 1, n6))


def _inproj_kernel(x_ref, mod_ref, nw_ref, w_ref, *out_refs, d):
    h = _norm_modulate(x_ref[...], nw_ref[...], mod_ref[:, 0:d], mod_ref[:, d:2 * d]).astype(BF16)
    for (a, b), o_ref in zip(SEGS, out_refs):
        o_ref[...] = jnp.dot(h, w_ref[:, a:b], preferred_element_type=F32)


def _in_projection(x, mod, nw, w_pad):
    b, t, d = x.shape
    tm = _pick(t, (512, 256, 128))
    return pl.pallas_call(
        functools.partial(_inproj_kernel, d=d),
        grid=(b, t // tm),
        in_specs=[pl.BlockSpec((None, tm, d), lambda i, j: (i, j, 0)),
                  pl.BlockSpec((None, 1, 6 * d), lambda i, j: (i, 0, 0)),
                  pl.BlockSpec((1, d), lambda i, j: (0, 0)),
                  pl.BlockSpec((d, IN_PAD), lambda i, j: (0, 0))],
        out_specs=[pl.BlockSpec((None, tm, s1 - s0), lambda i, j: (i, j, 0)) for s0, s1 in SEGS],
        out_shape=[jax.ShapeDtypeStruct((b, t, s1 - s0), F32) for s0, s1 in SEGS],
        compiler_params=_cparams(("parallel", "parallel")),
        name="norm_in_projection",
    )(x, mod, nw, w_pad)


def _dn_prep_kernel(xp_ref, x_ref, xn_ref, ab_ref, cw_ref, gp_ref, head_ones_ref, cum_f_ref, cum_b_ref,
                    expand_ref, q_ref, k_ref, v_ref, bf_ref, gf_ref, bb_ref, gb_ref, gc_ref, xe_ref, *, tm):
    i = pl.program_id(1)
    halo = SUBLANES
    pad = (DN_CONV_W - 1) // 2
    zeros = jnp.zeros((halo, 3 * DN_DIM), F32)
    xe_ref[0:halo, :] = jnp.where(i > 0, xp_ref[...], zeros)
    xe_ref[halo:halo + tm, :] = x_ref[...]
    xe_ref[halo + tm:halo + tm + halo, :] = jnp.where(i < pl.num_programs(1) - 1, xn_ref[...], zeros)
    acc = cw_ref[0:1, :] * xe_ref[halo - pad:halo - pad + tm, :]
    for j in range(1, DN_CONV_W):
        acc = acc + cw_ref[j:j + 1, :] * xe_ref[halo - pad + j:halo - pad + j + tm, :]
    qkv = _silu(acc)
    ones = head_ones_ref[...]
    q = qkv[:, 0:DN_DIM]
    k = qkv[:, DN_DIM:2 * DN_DIM]
    q_ref[...] = q * lax.rsqrt(_dot_sel(q * q, ones) + EPS) * (HEAD_DIM ** -0.5)
    k_ref[...] = k * lax.rsqrt(_dot_sel(k * k, ones) + EPS)
    v_ref[...] = qkv[:, 2 * DN_DIM:3 * DN_DIM]

    ab = ab_ref[...]
    lane = lax.broadcasted_iota(jnp.int32, ab.shape, 1)
    z = ab + gp_ref[1:2, :]
    softplus = jnp.maximum(z, 0.0) + jnp.log(1.0 + jnp.exp(-jnp.abs(z)))
    g = -jnp.exp(gp_ref[0:1, :]) * softplus
    g = jnp.where(lane < 2 * DN_HEADS, g, 0.0)
    gc = jnp.where(lane < DN_HEADS, _dot_sel_left(cum_f_ref[...], g), _dot_sel_left(cum_b_ref[...], g))
    gc_ref[...] = gc
    pieces = _split3(jnp.where(lane < 2 * DN_HEADS, gc, _sigmoid(ab)))
    for n, o_ref in enumerate((gf_ref, gb_ref, bf_ref, bb_ref)):
        e = expand_ref[n]
        o_ref[...] = sum(jnp.dot(p, e, preferred_element_type=F32) for p in pieces)


def _head_ones(width):
    idx = jnp.arange(width) // HEAD_DIM
    return (idx[:, None] == idx[None, :]).astype(BF16)


def _dn_prepare(qkv, ab, conv_w, a_log, dt_bias):
    b, t, c = qkv.shape
    tm = _pick(t, (256, 128, 64))
    nt = t // tm
    hb = tm // SUBLANES
    last8 = t // SUBLANES - 1
    gp = jnp.zeros((SUBLANES, LANES), F32)
    gp = gp.at[0, :2 * DN_HEADS].set(a_log.reshape(-1)).at[1, :2 * DN_HEADS].set(dt_bias.reshape(-1))
    r = jnp.arange(tm)
    same = (r[:, None] // DN_CHUNK) == (r[None, :] // DN_CHUNK)
    cum_f = (same & (r[None, :] <= r[:, None])).astype(BF16)
    cum_b = (same & (r[None, :] >= r[:, None])).astype(BF16)
    col_head = jnp.arange(DN_DIM) // HEAD_DIM
    src = jnp.arange(LANES)[:, None]
    expand = jnp.stack([(src == col_head[None, :] + off) for off in
                        (0, DN_HEADS, 2 * DN_HEADS, 3 * DN_HEADS)]).astype(BF16)
    wide = pl.BlockSpec((None, tm, DN_DIM), lambda i, j: (i, j, 0))
    const = lambda shape: pl.BlockSpec(shape, lambda i, j: (0,) * len(shape))
    outs = pl.pallas_call(
        functools.partial(_dn_prep_kernel, tm=tm),
        grid=(b, nt),
        in_specs=[pl.BlockSpec((None, SUBLANES, c), lambda i, j: (i, jnp.maximum(j * hb - 1, 0), 0)),
                  pl.BlockSpec((None, tm, c), lambda i, j: (i, j, 0)),
                  pl.BlockSpec((None, SUBLANES, c), lambda i, j: (i, jnp.minimum((j + 1) * hb, last8), 0)),
                  pl.BlockSpec((None, tm, LANES), lambda i, j: (i, j, 0)),
                  const((DN_CONV_W, c)), const((SUBLANES, LANES)), const((DN_DIM, DN_DIM)),
                  const((tm, tm)), const((tm, tm)), const((4, LANES, DN_DIM))],
        out_specs=[wide] * 7 + [pl.BlockSpec((None, tm, LANES), lambda i, j: (i, j, 0))],
        out_shape=[jax.ShapeDtypeStruct((b, t, DN_DIM), F32)] * 7 + [jax.ShapeDtypeStruct((b, t, LANES), F32)],
        scratch_shapes=[pltpu.VMEM((tm + 2 * SUBLANES, c), F32)],
        compiler_params=_cparams(("parallel", "parallel")),
        name="deltanet_prepare",
    )(qkv, qkv, qkv, ab, conv_w, gp, _head_ones(DN_DIM), cum_f, cum_b, expand)
    return outs


def _tile4(x):
    return jnp.concatenate([x] * HEADS_PER_GROUP, axis=0)


def _group_masks():
    n = HEADS_PER_GROUP * DN_CHUNK
    row = lax.broadcasted_iota(jnp.int32, (n, n), 0)
    col = lax.broadcasted_iota(jnp.int32, (n, n), 1)
    return row, col, (row // DN_CHUNK) == (col // DN_CHUNK)


def _collapse(full, same_head):
    kept = jnp.where(same_head, full, 0.0)
    out = kept[0:DN_CHUNK]
    for h in range(1, HEADS_PER_GROUP):
        out = out + kept[h * DN_CHUNK:(h + 1) * DN_CHUNK]
    return out


def _dn_local_kernel(q_ref, k_ref, v_ref, btf_ref, gxf_ref, btb_ref, gxb_ref, gc_ref, gr_ref,
                     uf_ref, wf_ref, af_ref, ub_ref, wb_ref, ab_ref, *, chunks):
    c = DN_CHUNK
    n = HEADS_PER_GROUP * c
    row, col, same_head = _group_masks()
    ri, cj = row % c, col % c
    eye = (row == col).astype(F32)
    tri = {False: (same_head & (ri >= cj), same_head & (ri > cj)),
           True: (same_head & (ri <= cj), same_head & (ri < cj))}

    def joins(s):
        return ((row // (2 * s)) == (col // (2 * s))) & ((row // s) != (col // s))

    chains = []
    for cc in range(chunks):
        rs = slice(cc * c, (cc + 1) * c)
        for d, (bt_ref, gx_ref, outs) in enumerate(((btf_ref, gxf_ref, (uf_ref, wf_ref, af_ref)),
                                                    (btb_ref, gxb_ref, (ub_ref, wb_ref, ab_ref)))):
            for g in range(DN_GROUPS):
                chains.append((rs, slice(g * MXU_DIM, (g + 1) * MXU_DIM), d, g, cc, bt_ref, gx_ref, outs))

    a_mats, rhs_list = [], []
    for rs, sl, d, g, cc, bt_ref, gx_ref, outs in chains:
        incl, strict = tri[d == 1]
        q, k, v = q_ref[rs, sl], k_ref[rs, sl], v_ref[rs, sl]
        beta, gcx = bt_ref[rs, sl], gx_ref[rs, sl]
        lane0 = d * DN_HEADS + g * HEADS_PER_GROUP
        col_b = jnp.concatenate([jnp.broadcast_to(gc_ref[rs, lane0 + h:lane0 + h + 1], (c, n))
                                 for h in range(HEADS_PER_GROUP)], axis=0)
        diff = col_b - jnp.broadcast_to(gr_ref[d, cc, :, sl], (n, n))
        decay = jnp.where(incl, jnp.exp(jnp.where(incl, diff, 0.0)), 0.0)
        k_beta = k * beta
        k_t = _tile4(k)
        a_mats.append(jnp.where(strict, _bdot_nt(jnp.where(same_head, _tile4(k_beta), 0.0), k_t) * decay, 0.0))
        attn = _bdot_nt(jnp.where(same_head, _tile4(q), 0.0), k_t) * decay
        outs[2][rs, sl] = _collapse(attn, same_head)
        rhs_list.append(jnp.concatenate([_tile4(v * beta), _tile4(k_beta * jnp.exp(gcx))], axis=1))

    invs = [eye - jnp.where(joins(1), a, 0.0) for a in a_mats]
    s = 2
    while s < c:
        js = joins(s)
        mids = [_bdot(jnp.where(js, a, 0.0), inv) for a, inv in zip(a_mats, invs)]
        invs = [inv - _bdot(inv, mid) for inv, mid in zip(invs, mids)]
        s *= 2

    for (rs, sl, d, g, cc, bt_ref, gx_ref, outs), inv, rhs in zip(chains, invs, rhs_list):
        sol = _bdot(inv, rhs)
        outs[0][rs, sl] = _collapse(sol[:, 0:n], same_head)
        outs[1][rs, sl] = _collapse(sol[:, n:2 * n], same_head)


def _dn_local(prep):
    q, k, v, beta_f, gcx_f, beta_b, gcx_b, gc = prep
    b, t, _ = q.shape
    n = t // DN_CHUNK
    chunks = 2 if n % 2 == 0 else 1
    rows = chunks * DN_CHUNK
    g_rows = gc[:, :, :2 * DN_HEADS].reshape(b, n, DN_CHUNK, 2, DN_HEADS)
    g_rows = g_rows.transpose(0, 3, 1, 4, 2).reshape(b, 2, n, 1, DN_DIM)
    wide = pl.BlockSpec((None, rows, DN_DIM), lambda i, j: (i, j, 0))
    return pl.pallas_call(
        functools.partial(_dn_local_kernel, chunks=chunks),
        grid=(b, n // chunks),
        in_specs=[wide] * 7 + [pl.BlockSpec((None, rows, LANES), lambda i, j: (i, j, 0)),
                               pl.BlockSpec((None, 2, chunks, 1, DN_DIM), lambda i, j: (i, 0, j, 0, 0))],
        out_specs=[wide] * 6,
        out_shape=[jax.ShapeDtypeStruct((b, t, DN_DIM), F32)] * 6,
        compiler_params=_cparams(("parallel", "parallel")),
        name="deltanet_chunk_local",
    )(q, k, v, beta_f, gcx_f, beta_b, gcx_b, gc, g_rows)


def _dn_scan_kernel(qf, kf, gxf, uf, wf, af, qb, kb, gxb, ub, wb, ab, s0_ref, of_ref, ob_ref, sfin_ref, s_ref,
                    *, chunks):
    step = pl.program_id(1)
    c = DN_CHUNK

    @pl.when(step == 0)
    def _():
        s_ref[...] = s0_ref[...]

    _, _, same_head = _group_masks()
    dirs = ((qf, kf, gxf, uf, wf, af, of_ref), (qb, kb, gxb, ub, wb, ab, ob_ref))
    states = {(d, g): s_ref[d, g] for d in range(2) for g in range(DN_GROUPS)}
    for cc in range(chunks):
        for d, (q, k, gx, u, w, at, o_ref) in enumerate(dirs):
            reverse = d == 1
            ck = chunks - 1 - cc if reverse else cc
            rs = slice(ck * c, (ck + 1) * c)
            for g in range(DN_GROUPS):
                sl = slice(g * MXU_DIM, (g + 1) * MXU_DIM)
                state = states[(d, g)]
                gcx = gx[rs, sl]
                g_last = gcx[0:1, :] if reverse else gcx[c - 1:c, :]
                q_dec = q[rs, sl] * jnp.exp(gcx)
                k_dec = k[rs, sl] * jnp.exp(g_last - gcx)
                ws = _bdot(jnp.concatenate([w[rs, sl], q_dec], axis=0), state)
                v_new = u[rs, sl] - ws[0:c]
                attn = jnp.where(same_head, _tile4(at[rs, sl]), 0.0)
                o_ref[rs, sl] = ws[c:2 * c] + _collapse(_bdot(attn, _tile4(v_new)), same_head)
                states[(d, g)] = state * jnp.exp(g_last) + jnp.where(same_head, _bdot_tn(k_dec, v_new), 0.0)
    for (d, g), state in states.items():
        s_ref[d, g] = state

    @pl.when(step == pl.num_programs(1) - 1)
    def _():
        sfin_ref[...] = s_ref[...]


def _dn_scan(prep, s0):
    q, k, _, _, gcx_f, _, gcx_b, _ = prep
    u_f, w_f, a_f, u_b, w_b, a_b = _dn_local(prep)
    b, t, _ = q.shape
    n = t // DN_CHUNK
    chunks = 2 if n % 2 == 0 else 1
    rows = chunks * DN_CHUNK
    steps = n // chunks
    fwd = pl.BlockSpec((None, rows, DN_DIM), lambda i, j: (i, j, 0))
    bwd = pl.BlockSpec((None, rows, DN_DIM), lambda i, j: (i, steps - 1 - j, 0))
    state = pl.BlockSpec((None, 2, DN_GROUPS, MXU_DIM, MXU_DIM), lambda i, j: (i, 0, 0, 0, 0))
    o_f, o_b, s_fin = pl.pallas_call(
        functools.partial(_dn_scan_kernel, chunks=chunks),
        grid=(b, steps),
        in_specs=[fwd] * 6 + [bwd] * 6 + [state],
        out_specs=[fwd, bwd, state],
        out_shape=[jax.ShapeDtypeStruct((b, t, DN_DIM), F32)] * 2
        + [jax.ShapeDtypeStruct((b, 2, DN_GROUPS, MXU_DIM, MXU_DIM), F32)],
        scratch_shapes=[pltpu.VMEM((2, DN_GROUPS, MXU_DIM, MXU_DIM), F32)],
        compiler_params=_cparams(("parallel", "arbitrary")),
        name="deltanet_scan",
    )(q, k, gcx_f, u_f, w_f, a_f, q, k, gcx_b, u_b, w_b, a_b, s0)
    return o_f, o_b, s_fin


def _rope(x, cos_t, sin_t):
    width = x.shape[-1]
    half = HEAD_DIM // 2
    lane = lax.broadcasted_iota(jnp.int32, x.shape, 1)
    partner = jnp.where(lane % HEAD_DIM < half, pltpu.roll(x, width - half, 1), pltpu.roll(x, half, 1))
    reps = width // LANES
    cos_w = jnp.concatenate([cos_t] * reps, axis=1) if reps > 1 else cos_t
    sin_w = jnp.concatenate([sin_t] * reps, axis=1) if reps > 1 else sin_t
    return x * cos_w + partner * sin_w


def _head_rmsnorm(x, ones, w):
    ms = _dot_sel(x * x, ones) * (1.0 / HEAD_DIM)
    return x * lax.rsqrt(ms + EPS) * w


def _attn_prep_kernel(q_ref, k_ref, v_ref, qw_ref, kw_ref, cos_ref, sin_ref, ones_q_ref, ones_k_ref,
                      qo_ref, ko_ref, vo_ref, *, rope):
    q = _head_rmsnorm(q_ref[...], ones_q_ref[...], qw_ref[...])
    k = _head_rmsnorm(k_ref[...], ones_k_ref[...], kw_ref[...])
    if rope:
        q = _rope(q, cos_ref[...], sin_ref[...])
        k = _rope(k, cos_ref[...], sin_ref[...])
    qo_ref[...] = (q * (HEAD_DIM ** -0.5 * LOG2E)).astype(BF16)
    v = v_ref[...]
    lane = lax.broadcasted_iota(jnp.int32, k.shape, 1)
    first = lane < HEAD_DIM
    k_sw = pltpu.roll(k, HEAD_DIM, 1)
    v_sw = pltpu.roll(v, HEAD_DIM, 1)
    ko_ref[0] = jnp.where(first, k, k_sw).astype(BF16)
    ko_ref[1] = jnp.where(first, k_sw, k).astype(BF16)
    vo_ref[0] = jnp.where(first, v, 1.0).T.astype(BF16)
    vo_ref[1] = jnp.where(first, v_sw, 1.0).T.astype(BF16)


def _rope_tables(t):
    rows = t // GRID_W
    row_pos = jnp.repeat(jnp.arange(rows, dtype=F32), GRID_W, total_repeat_length=t)
    col_pos = jnp.tile(jnp.arange(GRID_W, dtype=F32), rows)
    n_freq = HEAD_DIM // 4
    freqs = ROPE_THETA ** (-jnp.arange(n_freq, dtype=F32) / n_freq)
    ang = jnp.concatenate([row_pos[:, None] * freqs, col_pos[:, None] * freqs], axis=-1)
    cos, sin = jnp.cos(ang), jnp.sin(ang)
    reps = LANES // HEAD_DIM
    return (jnp.tile(jnp.concatenate([cos, cos], axis=-1), (1, reps)),
            jnp.tile(jnp.concatenate([-sin, sin], axis=-1), (1, reps)))


def _attn_prepare(atq, atk, atv, qn_w, kn_w, rope):
    b, t, _ = atq.shape
    tm = _pick(t, (512, 256, 128))
    if rope:
        cos_t, sin_t = _rope_tables(t)
    else:
        cos_t = sin_t = jnp.zeros((t, LANES), F32)
    qw = jnp.tile(qn_w, AT_Q_HEADS).reshape(1, AT_Q_DIM)
    kw = jnp.tile(kn_w, AT_KV_HEADS).reshape(1, AT_KV_DIM)
    const = lambda shape: pl.BlockSpec(shape, lambda i, j: (0,) * len(shape))
    tab = pl.BlockSpec((tm, LANES), lambda i, j: (j, 0))
    kv_out = pl.BlockSpec((None, AT_KV_HEADS, tm, AT_KV_DIM), lambda i, j: (i, 0, j, 0))
    return pl.pallas_call(
        functools.partial(_attn_prep_kernel, rope=rope),
        grid=(b, t // tm),
        in_specs=[pl.BlockSpec((None, tm, AT_Q_DIM), lambda i, j: (i, j, 0)),
                  pl.BlockSpec((None, tm, AT_KV_DIM), lambda i, j: (i, j, 0)),
                  pl.BlockSpec((None, tm, AT_KV_DIM), lambda i, j: (i, j, 0)),
                  const((1, AT_Q_DIM)), const((1, AT_KV_DIM)), tab, tab,
                  const((AT_Q_DIM, AT_Q_DIM)), const((AT_KV_DIM, AT_KV_DIM))],
        out_specs=[pl.BlockSpec((None, tm, AT_Q_DIM), lambda i, j: (i, j, 0)), kv_out,
                   pl.BlockSpec((None, AT_KV_HEADS, AT_KV_DIM, tm), lambda i, j: (i, 0, 0, j))],
        out_shape=[jax.ShapeDtypeStruct((b, t, AT_Q_DIM), BF16),
                   jax.ShapeDtypeStruct((b, AT_KV_HEADS, t, AT_KV_DIM), BF16),
                   jax.ShapeDtypeStruct((b, AT_KV_HEADS, AT_KV_DIM, t), BF16)],
        compiler_params=_cparams(("parallel", "parallel")),
        name="attention_prepare",
    )(atq, atk, atv, qw, kw, cos_t, sin_t, _head_ones(AT_Q_DIM), _head_ones(AT_KV_DIM))


def _flash_kernel(q_ref, k_ref, vt_ref, o_ref, qs_ref, m_ref, acc_ref, *, tq):
    j = pl.program_id(3)

    @pl.when(j == 0)
    def _():
        q = q_ref[...]
        lane = lax.broadcasted_iota(jnp.int32, q.shape, 1)
        for h in range(AT_GROUP):
            qs_ref[h * tq:(h + 1) * tq, :] = jnp.where(lane // HEAD_DIM == h, q, jnp.zeros_like(q))
        m_ref[...] = jnp.full(m_ref.shape, -jnp.inf, F32)
        acc_ref[...] = jnp.zeros(acc_ref.shape, F32)

    k = k_ref[...]
    k2 = jnp.concatenate([k, k], axis=1)
    vt = vt_ref[...]
    blocks = [slice(r0, r0 + FLASH_ROWS) for r0 in range(0, AT_GROUP * tq, FLASH_ROWS)]
    scores = [lax.dot_general(k2, qs_ref[cols, :], (((1,), (1,)), ((), ())), preferred_element_type=F32)
              for cols in blocks]
    m_prev = [m_ref[:, cols] for cols in blocks]
    acc_prev = [acc_ref[:, cols] for cols in blocks]
    m_next = [jnp.maximum(mp, jnp.max(s, axis=0, keepdims=True)) for mp, s in zip(m_prev, scores)]
    probs = [jnp.exp2(s - mn[0:1, :]).astype(BF16) for s, mn in zip(scores, m_next)]
    pv = [jnp.dot(vt, p, preferred_element_type=F32) for p in probs]
    for cols, mp, mn, ap, x in zip(blocks, m_prev, m_next, acc_prev, pv):
        acc_ref[:, cols] = jnp.exp2(mp[0:1, :] - mn[0:1, :]) * ap + x
        m_ref[:, cols] = mn

    @pl.when(j == pl.num_programs(3) - 1)
    def _():
        def normalised(h):
            a = acc_ref[:, h * tq:(h + 1) * tq]
            return a[0:HEAD_DIM, :] / a[HEAD_DIM:2 * HEAD_DIM, :]

        pairs = [jnp.concatenate([normalised(2 * p2), normalised(2 * p2 + 1)], axis=0).T
                 for p2 in range(AT_GROUP // 2)]
        o_ref[...] = jnp.concatenate(pairs, axis=1)


def _flash_attention(q, kd, vd):
    b, t, _ = q.shape
    s = kd.shape[2]
    tq = _pick(t, (1024, 512, 256, 128))
    tk = _pick(s, (768, 512, 256, 128))
    gw = AT_GROUP * HEAD_DIM
    return pl.pallas_call(
        functools.partial(_flash_kernel, tq=tq),
        grid=(b, AT_KV_HEADS, t // tq, s // tk),
        in_specs=[pl.BlockSpec((None, tq, gw), lambda i, g, a, j: (i, a, g)),
                  pl.BlockSpec((None, None, tk, AT_KV_DIM), lambda i, g, a, j: (i, g, j, 0)),
                  pl.BlockSpec((None, None, AT_KV_DIM, tk), lambda i, g, a, j: (i, g, 0, j))],
        out_specs=pl.BlockSpec((None, tq, gw), lambda i, g, a, j: (i, a, g)),
        out_shape=jax.ShapeDtypeStruct((b, t, AT_Q_DIM), F32),
        scratch_shapes=[pltpu.VMEM((AT_GROUP * tq, gw), BF16),
                        pltpu.VMEM((SUBLANES, AT_GROUP * tq), F32),
                        pltpu.VMEM((LANES, AT_GROUP * tq), F32)],
        compiler_params=_cparams(("parallel", "parallel", "parallel", "arbitrary")),
        name="gqa_flash_attention",
    )(q, kd, vd)


def _outproj_kernel(of_ref, ob_ref, z_ref, at_ref, x_ref, mod_ref, nw_ref, ones_ref, w_ref, o_ref, *, d):
    o = of_ref[...] + ob_ref[...]
    ms = _dot_sel(o * o, ones_ref[...]) * (1.0 / HEAD_DIM)
    dn = o * lax.rsqrt(ms + EPS) * nw_ref[...] * _silu(z_ref[...])
    mix = _bdot(dn, w_ref[0:DN_DIM, :]) + _bdot(at_ref[...], w_ref[DN_DIM:DN_DIM + AT_Q_DIM, :])
    o_ref[...] = x_ref[...] + mod_ref[:, 2 * d:3 * d] * mix


def _out_projection(o_f, o_b, z, at, x, mod, dn_norm_w, w_out):
    b, t, d = x.shape
    tm = _pick(t, (512, 256, 128))
    half = pl.BlockSpec((None, tm, DN_DIM), lambda i, j: (i, j, 0))
    full = pl.BlockSpec((None, tm, d), lambda i, j: (i, j, 0))
    const = lambda shape: pl.BlockSpec(shape, lambda i, j: (0,) * len(shape))
    return pl.pallas_call(
        functools.partial(_outproj_kernel, d=d),
        grid=(b, t // tm),
        in_specs=[half, half, half, half, full, pl.BlockSpec((None, 1, 6 * d), lambda i, j: (i, 0, 0)),
                  const((1, DN_DIM)), const((DN_DIM, DN_DIM)), const((DN_DIM + AT_Q_DIM, d))],
        out_specs=full,
        out_shape=jax.ShapeDtypeStruct((b, t, d), F32),
        compiler_params=_cparams(("parallel", "parallel")),
        name="mix_out_projection",
    )(o_f, o_b, z, at, x, mod, jnp.tile(dn_norm_w, DN_HEADS).reshape(1, DN_DIM), _head_ones(DN_DIM), w_out)


def _ffn_kernel(x_ref, mod_ref, nw_ref, fw_ref, wg_ref, wu_ref, wd_ref, o_ref, h_ref, acc_ref, *, d, final):
    f = pl.program_id(2)

    @pl.when(f == 0)
    def _():
        h_ref[...] = _norm_modulate(x_ref[...], nw_ref[...], mod_ref[:, 3 * d:4 * d],
                                    mod_ref[:, 4 * d:5 * d]).astype(BF16)
        acc_ref[...] = jnp.zeros(acc_ref.shape, F32)

    h = h_ref[...]
    gate = jnp.dot(h, wg_ref[...], preferred_element_type=F32)
    up = jnp.dot(h, wu_ref[...], preferred_element_type=F32)
    acc_ref[...] += jnp.dot((_silu(gate) * up).astype(BF16), wd_ref[...], preferred_element_type=F32)

    @pl.when(f == pl.num_programs(2) - 1)
    def _():
        y = x_ref[...] + mod_ref[:, 5 * d:6 * d] * acc_ref[...]
        if final:
            y = y * lax.rsqrt(jnp.mean(y * y, axis=-1, keepdims=True) + EPS) * fw_ref[...]
        o_ref[...] = y


def _dense_ffn(x, mod, nw, final_w, wg, wu, wd, final):
    b, t, d = x.shape
    fdim = wg.shape[1]
    tm = _pick(t, (1024, 512, 256, 128))
    tf = _pick(fdim, (512, 256, 128))
    full = pl.BlockSpec((None, tm, d), lambda i, j, f: (i, j, 0))
    const = lambda shape: pl.BlockSpec(shape, lambda i, j, f: (0,) * len(shape))
    return pl.pallas_call(
        functools.partial(_ffn_kernel, d=d, final=final),
        grid=(b, t // tm, fdim // tf),
        in_specs=[full, pl.BlockSpec((None, 1, 6 * d), lambda i, j, f: (i, 0, 0)), const((1, d)), const((1, d)),
                  pl.BlockSpec((d, tf), lambda i, j, f: (0, f)),
                  pl.BlockSpec((d, tf), lambda i, j, f: (0, f)),
                  pl.BlockSpec((tf, d), lambda i, j, f: (f, 0))],
        out_specs=full,
        out_shape=jax.ShapeDtypeStruct((b, t, d), F32),
        scratch_shapes=[pltpu.VMEM((tm, d), BF16), pltpu.VMEM((tm, d), F32)],
        compiler_params=_cparams(("parallel", "parallel", "arbitrary")),
        name="dense_swiglu",
    )(x, mod, nw, final_w, wg, wu, wd)


R_E1, R_E2, R_W1, R_W2, R_RANK1, R_RANK2 = range(6)


def _router_kernel(x_ref, mod_ref, nw_ref, rw_ref, rb_ref, tri_ref, h_ref, route_ref, cnt_ref, carry_ref, *, d):
    @pl.when(pl.program_id(0) == 0)
    def _():
        carry_ref[...] = jnp.zeros(carry_ref.shape, F32)

    h = _norm_modulate(x_ref[...], nw_ref[...], mod_ref[:, 3 * d:4 * d], mod_ref[:, 4 * d:5 * d])
    h_ref[...] = h
    logits = _bdot(h, rw_ref[...]) + rb_ref[...]
    lane = lax.broadcasted_iota(jnp.int32, logits.shape, 1).astype(F32)
    neg = jnp.float32(-jnp.inf)
    lg = jnp.where(lane < N_EXPERTS, logits, neg)
    m1 = jnp.max(lg, axis=1, keepdims=True)
    i1 = jnp.min(jnp.where(lg == m1, lane, float(LANES)), axis=1, keepdims=True)
    hot1 = lane == i1
    lg2 = jnp.where(hot1, neg, lg)
    m2 = jnp.max(lg2, axis=1, keepdims=True)
    i2 = jnp.min(jnp.where(lg2 == m2, lane, float(LANES)), axis=1, keepdims=True)
    hot2 = lane == i2
    e2 = jnp.exp(m2 - m1)
    w1 = 1.0 / (1.0 + e2)
    w2 = e2 / (1.0 + e2)
    member = jnp.where(hot1 | hot2, 1.0, 0.0)
    before = jnp.dot(tri_ref[...], member.astype(BF16), preferred_element_type=F32) + carry_ref[0:1, :]
    rank1 = jnp.sum(jnp.where(hot1, before, 0.0), axis=1, keepdims=True)
    rank2 = jnp.sum(jnp.where(hot2, before, 0.0), axis=1, keepdims=True)
    rec = jnp.zeros(logits.shape, F32)
    for ln, val in ((R_E1, i1), (R_E2, i2), (R_W1, w1), (R_W2, w2), (R_RANK1, rank1), (R_RANK2, rank2)):
        rec = jnp.where(lane == float(ln), val, rec)
    route_ref[...] = rec
    tb = member.shape[0]
    total = before[tb - 1:tb, :] + member[tb - 1:tb, :]
    carry_ref[...] = jnp.broadcast_to(total, carry_ref.shape)
    cnt_ref[...] = jnp.broadcast_to(total, cnt_ref.shape)


def _route(x2, mod, nw, router_w, router_b, rows_per_mod):
    n, d = x2.shape
    tb = _pick(rows_per_mod, (512, 256, 128))
    per = rows_per_mod // tb
    rw = jnp.zeros((d, LANES), BF16).at[:, :N_EXPERTS].set(router_w.astype(BF16))
    rb = jnp.zeros((1, LANES), F32).at[0, :N_EXPERTS].set(router_b)
    r = jnp.arange(tb)
    tri = (r[None, :] < r[:, None]).astype(BF16)
    const = lambda shape: pl.BlockSpec(shape, lambda i: (0,) * len(shape))
    return pl.pallas_call(
        functools.partial(_router_kernel, d=d),
        grid=(n // tb,),
        in_specs=[pl.BlockSpec((tb, d), lambda i: (i, 0)),
                  pl.BlockSpec((None, 1, 6 * d), lambda i: (i // per, 0, 0)),
                  const((1, d)), const((d, LANES)), const((1, LANES)), const((tb, tb))],
        out_specs=[pl.BlockSpec((tb, d), lambda i: (i, 0)), pl.BlockSpec((tb, LANES), lambda i: (i, 0)),
                   const((SUBLANES, LANES))],
        out_shape=[jax.ShapeDtypeStruct((n, d), F32), jax.ShapeDtypeStruct((n, LANES), F32),
                   jax.ShapeDtypeStruct((SUBLANES, LANES), F32)],
        scratch_shapes=[pltpu.VMEM((SUBLANES, LANES), F32)],
        compiler_params=_cparams(("arbitrary",)),
        name="moe_router_top2",
    )(x2, mod, nw, rw, rb, tri)


def _dispatch_kernel(dest_ref, h_ref, xs_in_ref, xs_ref, sem, *, tb):
    del xs_in_ref

    def copy(t, kk):
        return pltpu.make_async_copy(h_ref.at[pl.ds(t, 1), :],
                                     xs_ref.at[pl.ds(dest_ref[0, 2 * t + kk], 1), :], sem)

    def start(t, c):
        copy(t, 0).start(priority=0)
        copy(t, 1).start(priority=1)
        return c

    def wait(t, c):
        copy(t, 0).wait()
        copy(t, 1).wait()
        return c

    lax.fori_loop(0, tb, start, 0, unroll=DMA_UNROLL)
    lax.fori_loop(0, tb, wait, 0, unroll=DMA_UNROLL)


def _dispatch(h, dest, p_rows):
    n, d = h.shape
    tb = _pick(n, (256, 128))
    return pl.pallas_call(
        functools.partial(_dispatch_kernel, tb=tb),
        grid=(n // tb,),
        in_specs=[pl.BlockSpec((None, 1, 2 * tb), lambda i: (i, 0, 0), memory_space=pltpu.SMEM),
                  pl.BlockSpec((tb, d), lambda i: (i, 0)), pl.BlockSpec(memory_space=pl.ANY)],
        out_specs=pl.BlockSpec(memory_space=pl.ANY),
        out_shape=jax.ShapeDtypeStruct((p_rows, d), F32),
        scratch_shapes=[pltpu.SemaphoreType.DMA(())],
        input_output_aliases={2: 0},
        compiler_params=_cparams(("arbitrary",)),
        name="moe_dispatch_rows",
    )(dest.reshape(n // tb, 1, 2 * tb), h, jnp.zeros((p_rows, d), F32))


def _expert_kernel(be_ref, bx_ref, bv_ref, x_ref, wg_ref, wu_ref, wd_ref, o_ref, h_ref, acc_ref):
    i, f = pl.program_id(0), pl.program_id(1)
    valid = bv_ref[i] == 1
    last = f == pl.num_programs(1) - 1

    @pl.when(valid & (f == 0))
    def _():
        h_ref[...] = x_ref[...].astype(BF16)
        acc_ref[...] = jnp.zeros(acc_ref.shape, F32)

    @pl.when(valid)
    def _():
        h = h_ref[...]
        gate = jnp.dot(h, wg_ref[...], preferred_element_type=F32)
        up = jnp.dot(h, wu_ref[...], preferred_element_type=F32)
        acc_ref[...] += jnp.dot((_silu(gate) * up).astype(BF16), wd_ref[...], preferred_element_type=F32)

    @pl.when(valid & last)
    def _():
        o_ref[...] = acc_ref[...]

    @pl.when(jnp.logical_not(valid) & last)
    def _():
        o_ref[...] = jnp.zeros(o_ref.shape, F32)


def _experts(xs, blk_e, blk_x, blk_v, wg, wu, wd, mb):
    p_rows, d = xs.shape
    fdim = wg.shape[2]
    tf = _pick(fdim, (512, 256, 128))
    nf = fdim // tf
    fsel = lambda i, f, bv: jnp.where(bv[i] == 1, f, nf - 1)
    grid_spec = pltpu.PrefetchScalarGridSpec(
        num_scalar_prefetch=3,
        grid=(p_rows // mb, nf),
        in_specs=[pl.BlockSpec((mb, d), lambda i, f, be, bx, bv: (bx[i], 0)),
                  pl.BlockSpec((None, d, tf), lambda i, f, be, bx, bv: (be[i], 0, fsel(i, f, bv))),
                  pl.BlockSpec((None, d, tf), lambda i, f, be, bx, bv: (be[i], 0, fsel(i, f, bv))),
                  pl.BlockSpec((None, tf, d), lambda i, f, be, bx, bv: (be[i], fsel(i, f, bv), 0))],
        out_specs=pl.BlockSpec((mb, d), lambda i, f, be, bx, bv: (i, 0)),
        scratch_shapes=[pltpu.VMEM((mb, d), BF16), pltpu.VMEM((mb, d), F32)])
    return pl.pallas_call(
        _expert_kernel,
        grid_spec=grid_spec,
        out_shape=jax.ShapeDtypeStruct((p_rows, d), F32),
        compiler_params=_cparams(("arbitrary", "arbitrary")),
        name="moe_expert_swiglu",
    )(blk_e, blk_x, blk_v, xs, wg, wu, wd)


def _combine_kernel(dest_ref, ys_ref, x_ref, route_ref, mod_ref, fw_ref, o_ref, y1_ref, y2_ref, sem, *,
                    d, tb, final):
    def copy(t, kk, buf):
        return pltpu.make_async_copy(ys_ref.at[pl.ds(dest_ref[0, 2 * t + kk], 1), :],
                                     buf.at[pl.ds(t, 1), :], sem)

    def start(t, c):
        copy(t, 0, y1_ref).start(priority=0)
        copy(t, 1, y2_ref).start(priority=1)
        return c

    def wait(t, c):
        copy(t, 0, y1_ref).wait()
        copy(t, 1, y2_ref).wait()
        return c

    lax.fori_loop(0, tb, start, 0, unroll=DMA_UNROLL)
    lax.fori_loop(0, tb, wait, 0, unroll=DMA_UNROLL)
    w1 = route_ref[:, R_W1:R_W1 + 1]
    w2 = route_ref[:, R_W2:R_W2 + 1]
    y = x_ref[...] + mod_ref[:, 5 * d:6 * d] * (y1_ref[...] * w1 + y2_ref[...] * w2)
    if final:
        y = y * lax.rsqrt(jnp.mean(y * y, axis=-1, keepdims=True) + EPS) * fw_ref[...]
    o_ref[...] = y


def _combine(ys, dest, x2, route, mod, final_w, rows_per_mod, final):
    n, d = x2.shape
    tb = _pick(rows_per_mod, (256, 128))
    per = rows_per_mod // tb
    return pl.pallas_call(
        functools.partial(_combine_kernel, d=d, tb=tb, final=final),
        grid=(n // tb,),
        in_specs=[pl.BlockSpec((None, 1, 2 * tb), lambda i: (i, 0, 0), memory_space=pltpu.SMEM),
                  pl.BlockSpec(memory_space=pl.ANY),
                  pl.BlockSpec((tb, d), lambda i: (i, 0)),
                  pl.BlockSpec((tb, LANES), lambda i: (i, 0)),
                  pl.BlockSpec((None, 1, 6 * d), lambda i: (i // per, 0, 0)),
                  pl.BlockSpec((1, d), lambda i: (0, 0))],
        out_specs=pl.BlockSpec((tb, d), lambda i: (i, 0)),
        out_shape=jax.ShapeDtypeStruct((n, d), F32),
        scratch_shapes=[pltpu.VMEM((tb, d), F32), pltpu.VMEM((tb, d), F32), pltpu.SemaphoreType.DMA(())],
        compiler_params=_cparams(("arbitrary",)),
        name="moe_combine_rows",
    )(dest.reshape(n // tb, 1, 2 * tb), ys, x2, route, mod, final_w)


def _moe_ffn(x, mod, nw, final_w, router_w, router_b, wg, wu, wd, final):
    b, t, d = x.shape
    n = b * t
    x2 = x.reshape(n, d)
    h, route, counts = _route(x2, mod, nw, router_w, router_b, t)
    mb = _pick(n, (1024, 512, 256, 128))
    cnt = counts[0, :N_EXPERTS].astype(jnp.int32)
    nblk_e = (cnt + mb - 1) // mb
    blk_end = jnp.cumsum(nblk_e)
    pad_start = (blk_end - nblk_e) * mb
    nblk = (2 * n) // mb + N_EXPERTS
    bi = jnp.arange(nblk, dtype=jnp.int32)
    total_blk = blk_end[-1]
    blk_v = (bi < total_blk).astype(jnp.int32)
    clamped = jnp.minimum(bi, total_blk - 1).astype(jnp.int32)
    blk_e = jnp.sum((clamped[:, None] >= blk_end[None, :]).astype(jnp.int32), axis=1)
    blk_e = jnp.minimum(blk_e, N_EXPERTS - 1).astype(jnp.int32)
    e12 = route[:, R_E1:R_E2 + 1].astype(jnp.int32)
    rank12 = route[:, R_RANK1:R_RANK2 + 1].astype(jnp.int32)
    dest = (pad_start[e12] + rank12).reshape(n * 2)
    xs = _dispatch(h, dest, nblk * mb)
    ys = _experts(xs, blk_e, clamped, blk_v, wg, wu, wd, mb)
    out = _combine(ys, dest, x2, route, mod, final_w, t, final)
    return out.reshape(b, t, d)


def _pad_in_weight(w):
    d = w.shape[0]
    ab_end = 4 * DN_DIM + 4 * DN_HEADS
    return jnp.concatenate([w[:, :ab_end], jnp.zeros((d, SEG_AB[1] - ab_end), w.dtype), w[:, ab_end:]],
                           axis=1).astype(BF16)


def kernel(x, c, ctx, c_ctx, w_mod, b_mod, norm1_w, norm2_w, w_in, dn_conv_w, dn_a_log, dn_dt_bias, dn_norm_w, at_qnorm_w, at_knorm_w, w_out, ffn_w_gate, ffn_w_up, ffn_w_down, moe_router_w, moe_router_b, moe_w_gate, moe_w_up, moe_w_down, final_norm_w):
    b, t, d = x.shape
    depth = w_mod.shape[0]
    xc = ctx
    rows = -(-(b + 1) // SUBLANES) * SUBLANES
    c_rows = jnp.zeros((rows, d), F32).at[:b].set(c).at[b].set(c_ctx)
    mod_all = _modulation(c_rows, w_mod, b_mod)
    final_w = final_norm_w.reshape(1, d)
    zero_state = jnp.zeros((b, 2, DN_GROUPS, MXU_DIM, MXU_DIM), F32)

    for layer in range(depth):
        last = layer == depth - 1
        mod = mod_all[layer, :b].reshape(b, 1, 6 * d)
        mod_c = jnp.broadcast_to(mod_all[layer, b].reshape(1, 1, 6 * d), (b, 1, 6 * d))
        n1 = norm1_w[layer].reshape(1, d)
        n2 = norm2_w[layer].reshape(1, d)
        w_pad = _pad_in_weight(w_in[layer])
        w_o = w_out[layer].astype(BF16)

        qkv, z, ab, atq, atk, atv = _in_projection(x, mod, n1, w_pad)
        qkv_c, z_c, ab_c, atq_c, atk_c, atv_c = _in_projection(xc, mod_c, n1, w_pad)

        prep_c = _dn_prepare(qkv_c, ab_c, dn_conv_w[layer], dn_a_log[layer], dn_dt_bias[layer])
        of_c, ob_c, s_ctx = _dn_scan(prep_c, zero_state)
        prep = _dn_prepare(qkv, ab, dn_conv_w[layer], dn_a_log[layer], dn_dt_bias[layer])
        o_f, o_b, _ = _dn_scan(prep, s_ctx)

        q_l, kd_l, vd_l = _attn_prepare(atq, atk, atv, at_qnorm_w[layer], at_knorm_w[layer], rope=True)
        q_c, kd_c, vd_c = _attn_prepare(atq_c, atk_c, atv_c, at_qnorm_w[layer], at_knorm_w[layer], rope=False)
        at = _flash_attention(q_l, jnp.concatenate([kd_l, kd_c], axis=2), jnp.concatenate([vd_l, vd_c], axis=3))

        x = _out_projection(o_f, o_b, z, at, x, mod, dn_norm_w[layer], w_o)
        if not last:
            at_c = _flash_attention(q_c, kd_c, vd_c)
            xc = _out_projection(of_c, ob_c, z_c, at_c, xc, mod_c, dn_norm_w[layer], w_o)

        if layer % 2 == 0:
            i = layer // 2
            wg, wu, wd = (w[i].astype(BF16) for w in (ffn_w_gate, ffn_w_up, ffn_w_down))
            x = _dense_ffn(x, mod, n2, final_w, wg, wu, wd, last)
            if not last:
                xc = _dense_ffn(xc, mod_c, n2, final_w, wg, wu, wd, False)
        else:
            j = layer // 2
            wg, wu, wd = (w[j].astype(BF16) for w in (moe_w_gate, moe_w_up, moe_w_down))
            x = _moe_ffn(x, mod, n2, final_w, moe_router_w[j], moe_router_b[j], wg, wu, wd, last)
            if not last:
                xc = _moe_ffn(xc, mod_c, n2, final_w, moe_router_w[j], moe_router_b[j], wg, wu, wd, False)
    return x
```

```python
import functools
import math

import jax
import jax.numpy as jnp
from jax import lax
from jax.experimental import pallas as pl
from jax.experimental.pallas import tpu as pltpu

F32 = jnp.float32
BF16 = jnp.bfloat16
EPS = 1e-6
LOG2E = math.log2(math.e)

DN_HEADS = 8
HEAD_DIM = 64
DN_DIM = DN_HEADS * HEAD_DIM
DN_CONV_W = 5
DN_CHUNK = 64
AT_Q_HEADS = 8
AT_KV_HEADS = 2
AT_GROUP = AT_Q_HEADS // AT_KV_HEADS
AT_Q_DIM = AT_Q_HEADS * HEAD_DIM
AT_KV_DIM = AT_KV_HEADS * HEAD_DIM
GRID_W = 64
ROPE_THETA = 10000.0
N_EXPERTS = 8

LANES = 128
SUBLANES = 8
MXU_DIM = 256
VMEM_LIMIT = 52 * 1024 * 1024

HEADS_PER_GROUP = MXU_DIM // HEAD_DIM
DN_GROUPS = DN_HEADS // HEADS_PER_GROUP
FLASH_ROWS = 256
FLASH_BATCH = 8
PV_ROWS = HEAD_DIM + 16
DMA_UNROLL = 8

SEG_QKV = (0, 3 * DN_DIM)
SEG_Z = (3 * DN_DIM, 4 * DN_DIM)
SEG_AB = (4 * DN_DIM, 4 * DN_DIM + LANES)
SEG_ATQ = (SEG_AB[1], SEG_AB[1] + AT_Q_DIM)
SEG_ATK = (SEG_ATQ[1], SEG_ATQ[1] + AT_KV_DIM)
SEG_ATV = (SEG_ATK[1], SEG_ATK[1] + AT_KV_DIM)
IN_PAD = SEG_ATV[1]
SEGS = (SEG_QKV, SEG_Z, SEG_AB, SEG_ATQ, SEG_ATK, SEG_ATV)


def _cparams(sem):
    return pltpu.CompilerParams(dimension_semantics=sem, vmem_limit_bytes=VMEM_LIMIT)


def _bdot(a, b):
    return jnp.dot(a.astype(BF16), b.astype(BF16), preferred_element_type=F32)


def _bdot_nt(a, b):
    return lax.dot_general(a.astype(BF16), b.astype(BF16), (((1,), (1,)), ((), ())),
                           preferred_element_type=F32)


def _bdot_tn(a, b):
    return lax.dot_general(a.astype(BF16), b.astype(BF16), (((0,), (0,)), ((), ())),
                           preferred_element_type=F32)


def _split3(x):
    hi = x.astype(BF16)
    r1 = x - hi.astype(F32)
    mid = r1.astype(BF16)
    lo = (r1 - mid.astype(F32)).astype(BF16)
    return hi, mid, lo


def _dot_sel(x, sel):
    dot = functools.partial(jnp.dot, preferred_element_type=F32)
    hi, mid, lo = _split3(x)
    return dot(hi, sel) + dot(mid, sel) + dot(lo, sel)


def _dot_sel_left(sel, x):
    dot = functools.partial(jnp.dot, preferred_element_type=F32)
    hi, mid, lo = _split3(x)
    return dot(sel, hi) + dot(sel, mid) + dot(sel, lo)


def _sigmoid(x):
    return 1.0 / (1.0 + jnp.exp(-x))


def _silu(x):
    return x * _sigmoid(x)


def _norm_modulate(x, nw, shift, scale):
    y = x * lax.rsqrt(jnp.mean(x * x, axis=-1, keepdims=True) + EPS)
    return (y * nw) * (1.0 + scale) + shift


def _pick(n, options):
    for o in options:
        if n % o == 0:
            return o
    raise ValueError(f"no tile in {options} divides {n}")


def _mod_kernel(c_ref, w_ref, b_ref, o_ref):
    o_ref[...] = _bdot(_silu(c_ref[...]), w_ref[...]) + b_ref[...]


def _modulation(c_rows, w_mod, b_mod):
    nl, d, n6 = w_mod.shape
    rows = c_rows.shape[0]
    tn = _pick(n6, (1536, 1024, 512, 256, 128))
    return pl.pallas_call(
        _mod_kernel,
        grid=(nl, n6 // tn),
        in_specs=[pl.BlockSpec((rows, d), lambda l, j: (0, 0)),
                  pl.BlockSpec((None, d, tn), lambda l, j: (l, 0, j)),
                  pl.BlockSpec((None, 1, tn), lambda l, j: (l, 0, j))],
        out_specs=pl.BlockSpec((None, rows, tn), lambda l, j: (l, 0, j)),
        out_shape=jax.ShapeDtypeStruct((nl, rows, n6), F32),
        compiler_params=_cparams(("parallel", "parallel")),
        name="adaln_modulation",
    )(c_rows, w_mod, b_mod.reshape(nl, 1, n6))


def _inproj_kernel(x_ref, mod_ref, nw_ref, w_ref, *out_refs, d):
    h = _norm_modulate(x_ref[...], nw_ref[...], mod_ref[:, 0:d], mod_ref[:, d:2 * d]).astype(BF16)
    for (a, b), o_ref in zip(SEGS, out_refs):
        o_ref[...] = jnp.dot(h, w_ref[:, a:b], preferred_element_type=F32)


def _in_projection(x, mod, nw, w_pad):
    b, t, d = x.shape
    tm = _pick(t, (512, 256, 128))
    return pl.pallas_call(
        functools.partial(_inproj_kernel, d=d),
        grid=(b, t // tm),
        in_specs=[pl.BlockSpec((None, tm, d), lambda i, j: (i, j, 0)),
                  pl.BlockSpec((None, 1, 6 * d), lambda i, j: (i, 0, 0)),
                  pl.BlockSpec((1, d), lambda i, j: (0, 0)),
                  pl.BlockSpec((d, IN_PAD), lambda i, j: (0, 0))],
        out_specs=[pl.BlockSpec((None, tm, s1 - s0), lambda i, j: (i, j, 0)) for s0, s1 in SEGS],
        out_shape=[jax.ShapeDtypeStruct((b, t, s1 - s0), F32) for s0, s1 in SEGS],
        compiler_params=_cparams(("parallel", "parallel")),
        name="norm_in_projection",
    )(x, mod, nw, w_pad)


def _dn_prep_kernel(xp_ref, x_ref, xn_ref, ab_ref, cw_ref, gp_ref, head_ones_ref, cum_f_ref, cum_b_ref,
                    expand_ref, q_ref, k_ref, v_ref, bf_ref, gf_ref, bb_ref, gb_ref, gc_ref, xe_ref, *, tm):
    i = pl.program_id(1)
    halo = SUBLANES
    pad = (DN_CONV_W - 1) // 2
    zeros = jnp.zeros((halo, 3 * DN_DIM), F32)
    xe_ref[0:halo, :] = jnp.where(i > 0, xp_ref[...], zeros)
    xe_ref[halo:halo + tm, :] = x_ref[...]
    xe_ref[halo + tm:halo + tm + halo, :] = jnp.where(i < pl.num_programs(1) - 1, xn_ref[...], zeros)
    acc = cw_ref[0:1, :] * xe_ref[halo - pad:halo - pad + tm, :]
    for j in range(1, DN_CONV_W):
        acc = acc + cw_ref[j:j + 1, :] * xe_ref[halo - pad + j:halo - pad + j + tm, :]
    qkv = _silu(acc)
    ones = head_ones_ref[...]
    q = qkv[:, 0:DN_DIM]
    k = qkv[:, DN_DIM:2 * DN_DIM]
    q_ref[...] = q * lax.rsqrt(_dot_sel(q * q, ones) + EPS) * (HEAD_DIM ** -0.5)
    k_ref[...] = k * lax.rsqrt(_dot_sel(k * k, ones) + EPS)
    v_ref[...] = qkv[:, 2 * DN_DIM:3 * DN_DIM]

    ab = ab_ref[...]
    lane = lax.broadcasted_iota(jnp.int32, ab.shape, 1)
    z = ab + gp_ref[1:2, :]
    softplus = jnp.maximum(z, 0.0) + jnp.log(1.0 + jnp.exp(-jnp.abs(z)))
    g = -jnp.exp(gp_ref[0:1, :]) * softplus
    g = jnp.where(lane < 2 * DN_HEADS, g, 0.0)
    gc = jnp.where(lane < DN_HEADS, _dot_sel_left(cum_f_ref[...], g), _dot_sel_left(cum_b_ref[...], g))
    gc_ref[...] = gc
    pieces = _split3(jnp.where(lane < 2 * DN_HEADS, gc, _sigmoid(ab)))
    for n, o_ref in enumerate((gf_ref, gb_ref, bf_ref, bb_ref)):
        e = expand_ref[n]
        o_ref[...] = sum(jnp.dot(p, e, preferred_element_type=F32) for p in pieces)


def _head_ones(width):
    idx = jnp.arange(width) // HEAD_DIM
    return (idx[:, None] == idx[None, :]).astype(BF16)


def _dn_prepare(qkv, ab, conv_w, a_log, dt_bias):
    b, t, c = qkv.shape
    tm = _pick(t, (256, 128, 64))
    nt = t // tm
    hb = tm // SUBLANES
    last8 = t // SUBLANES - 1
    gp = jnp.zeros((SUBLANES, LANES), F32)
    gp = gp.at[0, :2 * DN_HEADS].set(a_log.reshape(-1)).at[1, :2 * DN_HEADS].set(dt_bias.reshape(-1))
    r = jnp.arange(tm)
    same = (r[:, None] // DN_CHUNK) == (r[None, :] // DN_CHUNK)
    cum_f = (same & (r[None, :] <= r[:, None])).astype(BF16)
    cum_b = (same & (r[None, :] >= r[:, None])).astype(BF16)
    col_head = jnp.arange(DN_DIM) // HEAD_DIM
    src = jnp.arange(LANES)[:, None]
    expand = jnp.stack([(src == col_head[None, :] + off) for off in
                        (0, DN_HEADS, 2 * DN_HEADS, 3 * DN_HEADS)]).astype(BF16)
    wide = pl.BlockSpec((None, tm, DN_DIM), lambda i, j: (i, j, 0))
    const = lambda shape: pl.BlockSpec(shape, lambda i, j: (0,) * len(shape))
    outs = pl.pallas_call(
        functools.partial(_dn_prep_kernel, tm=tm),
        grid=(b, nt),
        in_specs=[pl.BlockSpec((None, SUBLANES, c), lambda i, j: (i, jnp.maximum(j * hb - 1, 0), 0)),
                  pl.BlockSpec((None, tm, c), lambda i, j: (i, j, 0)),
                  pl.BlockSpec((None, SUBLANES, c), lambda i, j: (i, jnp.minimum((j + 1) * hb, last8), 0)),
                  pl.BlockSpec((None, tm, LANES), lambda i, j: (i, j, 0)),
                  const((DN_CONV_W, c)), const((SUBLANES, LANES)), const((DN_DIM, DN_DIM)),
                  const((tm, tm)), const((tm, tm)), const((4, LANES, DN_DIM))],
        out_specs=[wide] * 7 + [pl.BlockSpec((None, tm, LANES), lambda i, j: (i, j, 0))],
        out_shape=[jax.ShapeDtypeStruct((b, t, DN_DIM), F32)] * 7 + [jax.ShapeDtypeStruct((b, t, LANES), F32)],
        scratch_shapes=[pltpu.VMEM((tm + 2 * SUBLANES, c), F32)],
        compiler_params=_cparams(("parallel", "parallel")),
        name="deltanet_prepare",
    )(qkv, qkv, qkv, ab, conv_w, gp, _head_ones(DN_DIM), cum_f, cum_b, expand)
    return outs


def _tile4(x):
    return jnp.concatenate([x] * HEADS_PER_GROUP, axis=0)


def _group_masks():
    n = HEADS_PER_GROUP * DN_CHUNK
    row = lax.broadcasted_iota(jnp.int32, (n, n), 0)
    col = lax.broadcasted_iota(jnp.int32, (n, n), 1)
    return row, col, (row // DN_CHUNK) == (col // DN_CHUNK)


def _collapse(full, same_head):
    kept = jnp.where(same_head, full, 0.0)
    out = kept[0:DN_CHUNK]
    for h in range(1, HEADS_PER_GROUP):
        out = out + kept[h * DN_CHUNK:(h + 1) * DN_CHUNK]
    return out


def _dn_local_kernel(q_ref, k_ref, v_ref, btf_ref, gxf_ref, btb_ref, gxb_ref, gc_ref, gr_ref,
                     uf_ref, wf_ref, af_ref, ub_ref, wb_ref, ab_ref, *, chunks):
    c = DN_CHUNK
    n = HEADS_PER_GROUP * c
    row, col, same_head = _group_masks()
    ri, cj = row % c, col % c
    eye = (row == col).astype(F32)
    tri = {False: (same_head & (ri >= cj), same_head & (ri > cj)),
           True: (same_head & (ri <= cj), same_head & (ri < cj))}

    def joins(s):
        return ((row // (2 * s)) == (col // (2 * s))) & ((row // s) != (col // s))

    chains = []
    for cc in range(chunks):
        rs = slice(cc * c, (cc + 1) * c)
        for d, (bt_ref, gx_ref, outs) in enumerate(((btf_ref, gxf_ref, (uf_ref, wf_ref, af_ref)),
                                                    (btb_ref, gxb_ref, (ub_ref, wb_ref, ab_ref)))):
            for g in range(DN_GROUPS):
                chains.append((rs, slice(g * MXU_DIM, (g + 1) * MXU_DIM), d, g, cc, bt_ref, gx_ref, outs))

    a_mats, rhs_list = [], []
    for rs, sl, d, g, cc, bt_ref, gx_ref, outs in chains:
        incl, strict = tri[d == 1]
        q, k, v = q_ref[rs, sl], k_ref[rs, sl], v_ref[rs, sl]
        beta, gcx = bt_ref[rs, sl], gx_ref[rs, sl]
        lane0 = d * DN_HEADS + g * HEADS_PER_GROUP
        col_b = jnp.concatenate([jnp.broadcast_to(gc_ref[rs, lane0 + h:lane0 + h + 1], (c, n))
                                 for h in range(HEADS_PER_GROUP)], axis=0)
        diff = col_b - jnp.broadcast_to(gr_ref[d, cc, :, sl], (n, n))
        decay = jnp.where(incl, jnp.exp(jnp.where(incl, diff, 0.0)), 0.0)
        k_beta = k * beta
        k_t = _tile4(k)
        a_mats.append(jnp.where(strict, _bdot_nt(jnp.where(same_head, _tile4(k_beta), 0.0), k_t) * decay, 0.0))
        attn = _bdot_nt(jnp.where(same_head, _tile4(q), 0.0), k_t) * decay
        outs[2][rs, sl] = _collapse(attn, same_head)
        rhs_list.append(jnp.concatenate([_tile4(v * beta), _tile4(k_beta * jnp.exp(gcx))], axis=1))

    invs = [eye - jnp.where(joins(1), a, 0.0) for a in a_mats]
    s = 2
    while s < c:
        js = joins(s)
        mids = [_bdot(jnp.where(js, a, 0.0), inv) for a, inv in zip(a_mats, invs)]
        invs = [inv - _bdot(inv, mid) for inv, mid in zip(invs, mids)]
        s *= 2

    for (rs, sl, d, g, cc, bt_ref, gx_ref, outs), inv, rhs in zip(chains, invs, rhs_list):
        sol = _bdot(inv, rhs)
        outs[0][rs, sl] = _collapse(sol[:, 0:n], same_head)
        outs[1][rs, sl] = _collapse(sol[:, n:2 * n], same_head)


def _dn_local(prep):
    q, k, v, beta_f, gcx_f, beta_b, gcx_b, gc = prep
    b, t, _ = q.shape
    n = t // DN_CHUNK
    chunks = 4 if n % 4 == 0 else (2 if n % 2 == 0 else 1)
    rows = chunks * DN_CHUNK
    g_rows = gc[:, :, :2 * DN_HEADS].reshape(b, n, DN_CHUNK, 2, DN_HEADS)
    g_rows = g_rows.transpose(0, 3, 1, 4, 2).reshape(b, 2, n, 1, DN_DIM)
    wide = pl.BlockSpec((None, rows, DN_DIM), lambda i, j: (i, j, 0))
    return pl.pallas_call(
        functools.partial(_dn_local_kernel, chunks=chunks),
        grid=(b, n // chunks),
        in_specs=[wide] * 7 + [pl.BlockSpec((None, rows, LANES), lambda i, j: (i, j, 0)),
                               pl.BlockSpec((None, 2, chunks, 1, DN_DIM), lambda i, j: (i, 0, j, 0, 0))],
        out_specs=[wide] * 6,
        out_shape=[jax.ShapeDtypeStruct((b, t, DN_DIM), F32)] * 6,
        compiler_params=_cparams(("parallel", "parallel")),
        name="deltanet_chunk_local",
    )(q, k, v, beta_f, gcx_f, beta_b, gcx_b, gc, g_rows)


def _dn_scan_kernel(qf, kf, gxf, uf, wf, af, qb, kb, gxb, ub, wb, ab, s0_ref, of_ref, ob_ref, sfin_ref, s_ref,
                    *, chunks):
    step = pl.program_id(1)
    c = DN_CHUNK

    @pl.when(step == 0)
    def _():
        s_ref[...] = s0_ref[...]

    _, _, same_head = _group_masks()
    dirs = ((qf, kf, gxf, uf, wf, af, of_ref), (qb, kb, gxb, ub, wb, ab, ob_ref))
    states = {(d, g): s_ref[d, g] for d in range(2) for g in range(DN_GROUPS)}
    for cc in range(chunks):
        for d, (q, k, gx, u, w, at, o_ref) in enumerate(dirs):
            reverse = d == 1
            ck = chunks - 1 - cc if reverse else cc
            rs = slice(ck * c, (ck + 1) * c)
            for g in range(DN_GROUPS):
                sl = slice(g * MXU_DIM, (g + 1) * MXU_DIM)
                state = states[(d, g)]
                gcx = gx[rs, sl]
                g_last = gcx[0:1, :] if reverse else gcx[c - 1:c, :]
                q_dec = q[rs, sl] * jnp.exp(gcx)
                k_dec = k[rs, sl] * jnp.exp(g_last - gcx)
                ws = _bdot(jnp.concatenate([w[rs, sl], q_dec], axis=0), state)
                v_new = u[rs, sl] - ws[0:c]
                attn = jnp.where(same_head, _tile4(at[rs, sl]), 0.0)
                o_ref[rs, sl] = ws[c:2 * c] + _collapse(_bdot(attn, _tile4(v_new)), same_head)
                states[(d, g)] = state * jnp.exp(g_last) + jnp.where(same_head, _bdot_tn(k_dec, v_new), 0.0)
    for (d, g), state in states.items():
        s_ref[d, g] = state

    @pl.when(step == pl.num_programs(1) - 1)
    def _():
        sfin_ref[...] = s_ref[...]


def _dn_scan(prep, s0):
    q, k, _, _, gcx_f, _, gcx_b, _ = prep
    u_f, w_f, a_f, u_b, w_b, a_b = _dn_local(prep)
    b, t, _ = q.shape
    n = t // DN_CHUNK
    chunks = 4 if n % 4 == 0 else (2 if n % 2 == 0 else 1)
    rows = chunks * DN_CHUNK
    steps = n // chunks
    fwd = pl.BlockSpec((None, rows, DN_DIM), lambda i, j: (i, j, 0))
    bwd = pl.BlockSpec((None, rows, DN_DIM), lambda i, j: (i, steps - 1 - j, 0))
    state = pl.BlockSpec((None, 2, DN_GROUPS, MXU_DIM, MXU_DIM), lambda i, j: (i, 0, 0, 0, 0))
    o_f, o_b, s_fin = pl.pallas_call(
        functools.partial(_dn_scan_kernel, chunks=chunks),
        grid=(b, steps),
        in_specs=[fwd] * 6 + [bwd] * 6 + [state],
        out_specs=[fwd, bwd, state],
        out_shape=[jax.ShapeDtypeStruct((b, t, DN_DIM), F32)] * 2
        + [jax.ShapeDtypeStruct((b, 2, DN_GROUPS, MXU_DIM, MXU_DIM), F32)],
        scratch_shapes=[pltpu.VMEM((2, DN_GROUPS, MXU_DIM, MXU_DIM), F32)],
        compiler_params=_cparams(("parallel", "arbitrary")),
        name="deltanet_scan",
    )(q, k, gcx_f, u_f, w_f, a_f, q, k, gcx_b, u_b, w_b, a_b, s0)
    return o_f, o_b, s_fin


def _rope(x, cos_t, sin_t):
    width = x.shape[-1]
    half = HEAD_DIM // 2
    lane = lax.broadcasted_iota(jnp.int32, x.shape, 1)
    partner = jnp.where(lane % HEAD_DIM < half, pltpu.roll(x, width - half, 1), pltpu.roll(x, half, 1))
    reps = width // LANES
    cos_w = jnp.concatenate([cos_t] * reps, axis=1) if reps > 1 else cos_t
    sin_w = jnp.concatenate([sin_t] * reps, axis=1) if reps > 1 else sin_t
    return x * cos_w + partner * sin_w


def _head_rmsnorm(x, ones, w):
    ms = _dot_sel(x * x, ones) * (1.0 / HEAD_DIM)
    return x * lax.rsqrt(ms + EPS) * w


def _attn_prep_kernel(q_ref, k_ref, v_ref, qw_ref, kw_ref, cos_ref, sin_ref, ones_q_ref, ones_k_ref,
                      qo_ref, ko_ref, vo_ref, *, rope):
    q = _head_rmsnorm(q_ref[...], ones_q_ref[...], qw_ref[...])
    k = _head_rmsnorm(k_ref[...], ones_k_ref[...], kw_ref[...])
    if rope:
        q = _rope(q, cos_ref[...], sin_ref[...])
        k = _rope(k, cos_ref[...], sin_ref[...])
    qo_ref[...] = (q * (HEAD_DIM ** -0.5 * LOG2E)).astype(BF16)
    v = v_ref[...]
    lane = lax.broadcasted_iota(jnp.int32, k.shape, 1)
    first = lane < HEAD_DIM
    k_sw = pltpu.roll(k, HEAD_DIM, 1)
    v_sw = pltpu.roll(v, HEAD_DIM, 1)
    ko_ref[0] = jnp.where(first, k, k_sw).astype(BF16)
    ko_ref[1] = jnp.where(first, k_sw, k).astype(BF16)
    vo_ref[0] = jnp.where(first, v, 1.0).T[0:PV_ROWS, :].astype(BF16)
    vo_ref[1] = jnp.where(first, v_sw, 1.0).T[0:PV_ROWS, :].astype(BF16)


def _rope_tables(t):
    rows = t // GRID_W
    row_pos = jnp.repeat(jnp.arange(rows, dtype=F32), GRID_W, total_repeat_length=t)
    col_pos = jnp.tile(jnp.arange(GRID_W, dtype=F32), rows)
    n_freq = HEAD_DIM // 4
    freqs = ROPE_THETA ** (-jnp.arange(n_freq, dtype=F32) / n_freq)
    ang = jnp.concatenate([row_pos[:, None] * freqs, col_pos[:, None] * freqs], axis=-1)
    cos, sin = jnp.cos(ang), jnp.sin(ang)
    reps = LANES // HEAD_DIM
    return (jnp.tile(jnp.concatenate([cos, cos], axis=-1), (1, reps)),
            jnp.tile(jnp.concatenate([-sin, sin], axis=-1), (1, reps)))


def _attn_prepare(atq, atk, atv, qn_w, kn_w, rope):
    b, t, _ = atq.shape
    tm = _pick(t, (512, 256, 128))
    if rope:
        cos_t, sin_t = _rope_tables(t)
    else:
        cos_t = sin_t = jnp.zeros((t, LANES), F32)
    qw = jnp.tile(qn_w, AT_Q_HEADS).reshape(1, AT_Q_DIM)
    kw = jnp.tile(kn_w, AT_KV_HEADS).reshape(1, AT_KV_DIM)
    const = lambda shape: pl.BlockSpec(shape, lambda i, j: (0,) * len(shape))
    tab = pl.BlockSpec((tm, LANES), lambda i, j: (j, 0))
    kv_out = pl.BlockSpec((None, AT_KV_HEADS, tm, AT_KV_DIM), lambda i, j: (i, 0, j, 0))
    return pl.pallas_call(
        functools.partial(_attn_prep_kernel, rope=rope),
        grid=(b, t // tm),
        in_specs=[pl.BlockSpec((None, tm, AT_Q_DIM), lambda i, j: (i, j, 0)),
                  pl.BlockSpec((None, tm, AT_KV_DIM), lambda i, j: (i, j, 0)),
                  pl.BlockSpec((None, tm, AT_KV_DIM), lambda i, j: (i, j, 0)),
                  const((1, AT_Q_DIM)), const((1, AT_KV_DIM)), tab, tab,
                  const((AT_Q_DIM, AT_Q_DIM)), const((AT_KV_DIM, AT_KV_DIM))],
        out_specs=[pl.BlockSpec((None, tm, AT_Q_DIM), lambda i, j: (i, j, 0)), kv_out,
                   pl.BlockSpec((None, AT_KV_HEADS, PV_ROWS, tm), lambda i, j: (i, 0, 0, j))],
        out_shape=[jax.ShapeDtypeStruct((b, t, AT_Q_DIM), BF16),
                   jax.ShapeDtypeStruct((b, AT_KV_HEADS, t, AT_KV_DIM), BF16),
                   jax.ShapeDtypeStruct((b, AT_KV_HEADS, PV_ROWS, t), BF16)],
        compiler_params=_cparams(("parallel", "parallel")),
        name="attention_prepare",
    )(atq, atk, atv, qw, kw, cos_t, sin_t, _head_ones(AT_Q_DIM), _head_ones(AT_KV_DIM))


def _flash_kernel(q_ref, k_ref, vt_ref, o_ref, qs_ref, m_ref, acc_ref, *, tq):
    j = pl.program_id(3)

    @pl.when(j == 0)
    def _():
        q = q_ref[...]
        lane = lax.broadcasted_iota(jnp.int32, q.shape, 1)
        for h in range(AT_GROUP):
            qs_ref[h * tq:(h + 1) * tq, :] = jnp.where(lane // HEAD_DIM == h, q, jnp.zeros_like(q))
        m_ref[...] = jnp.full(m_ref.shape, -jnp.inf, F32)
        acc_ref[...] = jnp.zeros(acc_ref.shape, F32)

    k = k_ref[...]
    k2 = jnp.concatenate([k, k], axis=1)
    vt = vt_ref[...]
    all_blocks = [slice(r0, r0 + FLASH_ROWS) for r0 in range(0, AT_GROUP * tq, FLASH_ROWS)]
    for b0 in range(0, len(all_blocks), FLASH_BATCH):
        blocks = all_blocks[b0:b0 + FLASH_BATCH]
        scores = [lax.dot_general(k2, qs_ref[cols, :], (((1,), (1,)), ((), ())), preferred_element_type=F32)
                  for cols in blocks]
        m_prev = [m_ref[:, cols] for cols in blocks]
        acc_prev = [acc_ref[:, cols] for cols in blocks]
        m_next = [jnp.maximum(mp, jnp.max(s, axis=0, keepdims=True)) for mp, s in zip(m_prev, scores)]
        probs = [jnp.exp2(s - mn[0:1, :]).astype(BF16) for s, mn in zip(scores, m_next)]
        pv = [jnp.dot(vt, p, preferred_element_type=F32) for p in probs]
        for cols, mp, mn, ap, x in zip(blocks, m_prev, m_next, acc_prev, pv):
            acc_ref[:, cols] = jnp.exp2(mp[0:1, :] - mn[0:1, :]) * ap + x
            m_ref[:, cols] = mn

    @pl.when(j == pl.num_programs(3) - 1)
    def _():
        def normalised(h):
            a = acc_ref[:, h * tq:(h + 1) * tq]
            return a[0:HEAD_DIM, :] / a[HEAD_DIM:HEAD_DIM + 1, :]

        pairs = [jnp.concatenate([normalised(2 * p2), normalised(2 * p2 + 1)], axis=0).T
                 for p2 in range(AT_GROUP // 2)]
        o_ref[...] = jnp.concatenate(pairs, axis=1)


def _flash_attention(q, kd, vd):
    b, t, _ = q.shape
    s = kd.shape[2]
    tq = _pick(t, (1024, 512, 256, 128))
    tk = _pick(s, (1408, 768, 512, 256, 128))
    gw = AT_GROUP * HEAD_DIM
    return pl.pallas_call(
        functools.partial(_flash_kernel, tq=tq),
        grid=(b, AT_KV_HEADS, t // tq, s // tk),
        in_specs=[pl.BlockSpec((None, tq, gw), lambda i, g, a, j: (i, a, g)),
                  pl.BlockSpec((None, None, tk, AT_KV_DIM), lambda i, g, a, j: (i, g, j, 0)),
                  pl.BlockSpec((None, None, PV_ROWS, tk), lambda i, g, a, j: (i, g, 0, j))],
        out_specs=pl.BlockSpec((None, tq, gw), lambda i, g, a, j: (i, a, g)),
        out_shape=jax.ShapeDtypeStruct((b, t, AT_Q_DIM), F32),
        scratch_shapes=[pltpu.VMEM((AT_GROUP * tq, gw), BF16),
                        pltpu.VMEM((SUBLANES, AT_GROUP * tq), F32),
                        pltpu.VMEM((PV_ROWS, AT_GROUP * tq), F32)],
        compiler_params=_cparams(("parallel", "parallel", "parallel", "arbitrary")),
        name="gqa_flash_attention",
    )(q, kd, vd)


def _outproj_kernel(of_ref, ob_ref, z_ref, at_ref, x_ref, mod_ref, nw_ref, ones_ref, w_ref, o_ref, *, d):
    o = of_ref[...] + ob_ref[...]
    ms = _dot_sel(o * o, ones_ref[...]) * (1.0 / HEAD_DIM)
    dn = o * lax.rsqrt(ms + EPS) * nw_ref[...] * _silu(z_ref[...])
    mix = _bdot(dn, w_ref[0:DN_DIM, :]) + _bdot(at_ref[...], w_ref[DN_DIM:DN_DIM + AT_Q_DIM, :])
    o_ref[...] = x_ref[...] + mod_ref[:, 2 * d:3 * d] * mix


def _out_projection(o_f, o_b, z, at, x, mod, dn_norm_w, w_out):
    b, t, d = x.shape
    tm = _pick(t, (512, 256, 128))
    half = pl.BlockSpec((None, tm, DN_DIM), lambda i, j: (i, j, 0))
    full = pl.BlockSpec((None, tm, d), lambda i, j: (i, j, 0))
    const = lambda shape: pl.BlockSpec(shape, lambda i, j: (0,) * len(shape))
    return pl.pallas_call(
        functools.partial(_outproj_kernel, d=d),
        grid=(b, t // tm),
        in_specs=[half, half, half, half, full, pl.BlockSpec((None, 1, 6 * d), lambda i, j: (i, 0, 0)),
                  const((1, DN_DIM)), const((DN_DIM, DN_DIM)), const((DN_DIM + AT_Q_DIM, d))],
        out_specs=full,
        out_shape=jax.ShapeDtypeStruct((b, t, d), F32),
        compiler_params=_cparams(("parallel", "parallel")),
        name="mix_out_projection",
    )(o_f, o_b, z, at, x, mod, jnp.tile(dn_norm_w, DN_HEADS).reshape(1, DN_DIM), _head_ones(DN_DIM), w_out)


def _ffn_kernel(x_ref, mod_ref, nw_ref, fw_ref, wg_ref, wu_ref, wd_ref, o_ref, h_ref, acc_ref, *, d, final):
    f = pl.program_id(2)

    @pl.when(f == 0)
    def _():
        h_ref[...] = _norm_modulate(x_ref[...], nw_ref[...], mod_ref[:, 3 * d:4 * d],
                                    mod_ref[:, 4 * d:5 * d]).astype(BF16)
        acc_ref[...] = jnp.zeros(acc_ref.shape, F32)

    h = h_ref[...]
    gate = jnp.dot(h, wg_ref[...], preferred_element_type=F32)
    up = jnp.dot(h, wu_ref[...], preferred_element_type=F32)
    acc_ref[...] += jnp.dot((_silu(gate) * up).astype(BF16), wd_ref[...], preferred_element_type=F32)

    @pl.when(f == pl.num_programs(2) - 1)
    def _():
        y = x_ref[...] + mod_ref[:, 5 * d:6 * d] * acc_ref[...]
        if final:
            y = y * lax.rsqrt(jnp.mean(y * y, axis=-1, keepdims=True) + EPS) * fw_ref[...]
        o_ref[...] = y


def _dense_ffn(x, mod, nw, final_w, wg, wu, wd, final):
    b, t, d = x.shape
    fdim = wg.shape[1]
    tm = _pick(t, (1024, 512, 256, 128))
    tf = _pick(fdim, (512, 256, 128))
    full = pl.BlockSpec((None, tm, d), lambda i, j, f: (i, j, 0))
    const = lambda shape: pl.BlockSpec(shape, lambda i, j, f: (0,) * len(shape))
    return pl.pallas_call(
        functools.partial(_ffn_kernel, d=d, final=final),
        grid=(b, t // tm, fdim // tf),
        in_specs=[full, pl.BlockSpec((None, 1, 6 * d), lambda i, j, f: (i, 0, 0)), const((1, d)), const((1, d)),
                  pl.BlockSpec((d, tf), lambda i, j, f: (0, f)),
                  pl.BlockSpec((d, tf), lambda i, j, f: (0, f)),
                  pl.BlockSpec((tf, d), lambda i, j, f: (f, 0))],
        out_specs=full,
        out_shape=jax.ShapeDtypeStruct((b, t, d), F32),
        scratch_shapes=[pltpu.VMEM((tm, d), BF16), pltpu.VMEM((tm, d), F32)],
        compiler_params=_cparams(("parallel", "parallel", "arbitrary")),
        name="dense_swiglu",
    )(x, mod, nw, final_w, wg, wu, wd)


R_E1, R_E2, R_W1, R_W2, R_RANK1, R_RANK2 = range(6)


def _router_kernel(x_ref, mod_ref, nw_ref, rw_ref, rb_ref, tri_ref, h_ref, route_ref, cnt_ref, carry_ref, *, d):
    @pl.when(pl.program_id(0) == 0)
    def _():
        carry_ref[...] = jnp.zeros(carry_ref.shape, F32)

    h = _norm_modulate(x_ref[...], nw_ref[...], mod_ref[:, 3 * d:4 * d], mod_ref[:, 4 * d:5 * d])
    h_ref[...] = h
    logits = _bdot(h, rw_ref[...]) + rb_ref[...]
    lane = lax.broadcasted_iota(jnp.int32, logits.shape, 1).astype(F32)
    neg = jnp.float32(-jnp.inf)
    lg = jnp.where(lane < N_EXPERTS, logits, neg)
    m1 = jnp.max(lg, axis=1, keepdims=True)
    i1 = jnp.min(jnp.where(lg == m1, lane, float(LANES)), axis=1, keepdims=True)
    hot1 = lane == i1
    lg2 = jnp.where(hot1, neg, lg)
    m2 = jnp.max(lg2, axis=1, keepdims=True)
    i2 = jnp.min(jnp.where(lg2 == m2, lane, float(LANES)), axis=1, keepdims=True)
    hot2 = lane == i2
    e2 = jnp.exp(m2 - m1)
    w1 = 1.0 / (1.0 + e2)
    w2 = e2 / (1.0 + e2)
    member = jnp.where(hot1 | hot2, 1.0, 0.0)
    before = jnp.dot(tri_ref[...], member.astype(BF16), preferred_element_type=F32) + carry_ref[0:1, :]
    rank1 = jnp.sum(jnp.where(hot1, before, 0.0), axis=1, keepdims=True)
    rank2 = jnp.sum(jnp.where(hot2, before, 0.0), axis=1, keepdims=True)
    rec = jnp.zeros(logits.shape, F32)
    for ln, val in ((R_E1, i1), (R_E2, i2), (R_W1, w1), (R_W2, w2), (R_RANK1, rank1), (R_RANK2, rank2)):
        rec = jnp.where(lane == float(ln), val, rec)
    route_ref[...] = rec
    tb = member.shape[0]
    total = before[tb - 1:tb, :] + member[tb - 1:tb, :]
    carry_ref[...] = jnp.broadcast_to(total, carry_ref.shape)
    cnt_ref[...] = jnp.broadcast_to(total, cnt_ref.shape)


def _route(x2, mod, nw, router_w, router_b, rows_per_mod):
    n, d = x2.shape
    tb = _pick(rows_per_mod, (512, 256, 128))
    per = rows_per_mod // tb
    rw = jnp.zeros((d, LANES), BF16).at[:, :N_EXPERTS].set(router_w.astype(BF16))
    rb = jnp.zeros((1, LANES), F32).at[0, :N_EXPERTS].set(router_b)
    r = jnp.arange(tb)
    tri = (r[None, :] < r[:, None]).astype(BF16)
    const = lambda shape: pl.BlockSpec(shape, lambda i: (0,) * len(shape))
    return pl.pallas_call(
        functools.partial(_router_kernel, d=d),
        grid=(n // tb,),
        in_specs=[pl.BlockSpec((tb, d), lambda i: (i, 0)),
                  pl.BlockSpec((None, 1, 6 * d), lambda i: (i // per, 0, 0)),
                  const((1, d)), const((d, LANES)), const((1, LANES)), const((tb, tb))],
        out_specs=[pl.BlockSpec((tb, d), lambda i: (i, 0)), pl.BlockSpec((tb, LANES), lambda i: (i, 0)),
                   const((SUBLANES, LANES))],
        out_shape=[jax.ShapeDtypeStruct((n, d), F32), jax.ShapeDtypeStruct((n, LANES), F32),
                   jax.ShapeDtypeStruct((SUBLANES, LANES), F32)],
        scratch_shapes=[pltpu.VMEM((SUBLANES, LANES), F32)],
        compiler_params=_cparams(("arbitrary",)),
        name="moe_router_top2",
    )(x2, mod, nw, rw, rb, tri)


def _dispatch_kernel(dest_ref, h_ref, xs_in_ref, xs_ref, sem, *, tb):
    del xs_in_ref

    def copy(t, kk):
        return pltpu.make_async_copy(h_ref.at[pl.ds(t, 1), :],
                                     xs_ref.at[pl.ds(dest_ref[0, 2 * t + kk], 1), :], sem)

    def start(t, c):
        copy(t, 0).start(priority=0)
        copy(t, 1).start(priority=1)
        return c

    def wait(t, c):
        copy(t, 0).wait()
        copy(t, 1).wait()
        return c

    lax.fori_loop(0, tb, start, 0, unroll=DMA_UNROLL)
    lax.fori_loop(0, tb, wait, 0, unroll=DMA_UNROLL)


def _dispatch(h, dest, p_rows):
    n, d = h.shape
    tb = _pick(n, (256, 128))
    return pl.pallas_call(
        functools.partial(_dispatch_kernel, tb=tb),
        grid=(n // tb,),
        in_specs=[pl.BlockSpec((None, 1, 2 * tb), lambda i: (i, 0, 0), memory_space=pltpu.SMEM),
                  pl.BlockSpec((tb, d), lambda i: (i, 0)), pl.BlockSpec(memory_space=pl.ANY)],
        out_specs=pl.BlockSpec(memory_space=pl.ANY),
        out_shape=jax.ShapeDtypeStruct((p_rows, d), F32),
        scratch_shapes=[pltpu.SemaphoreType.DMA(())],
        input_output_aliases={2: 0},
        compiler_params=_cparams(("arbitrary",)),
        name="moe_dispatch_rows",
    )(dest.reshape(n // tb, 1, 2 * tb), h, jnp.zeros((p_rows, d), F32))


def _expert_kernel(be_ref, bx_ref, bv_ref, x_ref, wg_ref, wu_ref, wd_ref, o_ref, h_ref, acc_ref):
    i, f = pl.program_id(0), pl.program_id(1)
    valid = bv_ref[i] == 1
    last = f == pl.num_programs(1) - 1

    @pl.when(valid & (f == 0))
    def _():
        h_ref[...] = x_ref[...].astype(BF16)
        acc_ref[...] = jnp.zeros(acc_ref.shape, F32)

    @pl.when(valid)
    def _():
        h = h_ref[...]
        gate = jnp.dot(h, wg_ref[...], preferred_element_type=F32)
        up = jnp.dot(h, wu_ref[...], preferred_element_type=F32)
        acc_ref[...] += jnp.dot((_silu(gate) * up).astype(BF16), wd_ref[...], preferred_element_type=F32)

    @pl.when(valid & last)
    def _():
        o_ref[...] = acc_ref[...]

    @pl.when(jnp.logical_not(valid) & last)
    def _():
        o_ref[...] = jnp.zeros(o_ref.shape, F32)


def _experts(xs, blk_e, blk_x, blk_v, wg, wu, wd, mb):
    p_rows, d = xs.shape
    fdim = wg.shape[2]
    tf = _pick(fdim, (512, 256, 128))
    nf = fdim // tf
    fsel = lambda i, f, bv: jnp.where(bv[i] == 1, f, nf - 1)
    grid_spec = pltpu.PrefetchScalarGridSpec(
        num_scalar_prefetch=3,
        grid=(p_rows // mb, nf),
        in_specs=[pl.BlockSpec((mb, d), lambda i, f, be, bx, bv: (bx[i], 0)),
                  pl.BlockSpec((None, d, tf), lambda i, f, be, bx, bv: (be[i], 0, fsel(i, f, bv))),
                  pl.BlockSpec((None, d, tf), lambda i, f, be, bx, bv: (be[i], 0, fsel(i, f, bv))),
                  pl.BlockSpec((None, tf, d), lambda i, f, be, bx, bv: (be[i], fsel(i, f, bv), 0))],
        out_specs=pl.BlockSpec((mb, d), lambda i, f, be, bx, bv: (i, 0)),
        scratch_shapes=[pltpu.VMEM((mb, d), BF16), pltpu.VMEM((mb, d), F32)])
    return pl.pallas_call(
        _expert_kernel,
        grid_spec=grid_spec,
        out_shape=jax.ShapeDtypeStruct((p_rows, d), F32),
        compiler_params=_cparams(("arbitrary", "arbitrary")),
        name="moe_expert_swiglu",
    )(blk_e, blk_x, blk_v, xs, wg, wu, wd)


def _combine_kernel(dest_ref, ys_ref, x_ref, route_ref, mod_ref, fw_ref, o_ref, y1_ref, y2_ref, sem, *,
                    d, tb, final):
    def copy(t, kk, buf):
        return pltpu.make_async_copy(ys_ref.at[pl.ds(dest_ref[0, 2 * t + kk], 1), :],
                                     buf.at[pl.ds(t, 1), :], sem)

    def start(t, c):
        copy(t, 0, y1_ref).start(priority=0)
        copy(t, 1, y2_ref).start(priority=1)
        return c

    def wait(t, c):
        copy(t, 0, y1_ref).wait()
        copy(t, 1, y2_ref).wait()
        return c

    lax.fori_loop(0, tb, start, 0, unroll=DMA_UNROLL)
    lax.fori_loop(0, tb, wait, 0, unroll=DMA_UNROLL)
    w1 = route_ref[:, R_W1:R_W1 + 1]
    w2 = route_ref[:, R_W2:R_W2 + 1]
    y = x_ref[...] + mod_ref[:, 5 * d:6 * d] * (y1_ref[...] * w1 + y2_ref[...] * w2)
    if final:
        y = y * lax.rsqrt(jnp.mean(y * y, axis=-1, keepdims=True) + EPS) * fw_ref[...]
    o_ref[...] = y


def _combine(ys, dest, x2, route, mod, final_w, rows_per_mod, final):
    n, d = x2.shape
    tb = _pick(rows_per_mod, (256, 128))
    per = rows_per_mod // tb
    return pl.pallas_call(
        functools.partial(_combine_kernel, d=d, tb=tb, final=final),
        grid=(n // tb,),
        in_specs=[pl.BlockSpec((None, 1, 2 * tb), lambda i: (i, 0, 0), memory_space=pltpu.SMEM),
                  pl.BlockSpec(memory_space=pl.ANY),
                  pl.BlockSpec((tb, d), lambda i: (i, 0)),
                  pl.BlockSpec((tb, LANES), lambda i: (i, 0)),
                  pl.BlockSpec((None, 1, 6 * d), lambda i: (i // per, 0, 0)),
                  pl.BlockSpec((1, d), lambda i: (0, 0))],
        out_specs=pl.BlockSpec((tb, d), lambda i: (i, 0)),
        out_shape=jax.ShapeDtypeStruct((n, d), F32),
        scratch_shapes=[pltpu.VMEM((tb, d), F32), pltpu.VMEM((tb, d), F32), pltpu.SemaphoreType.DMA(())],
        compiler_params=_cparams(("arbitrary",)),
        name="moe_combine_rows",
    )(dest.reshape(n // tb, 1, 2 * tb), ys, x2, route, mod, final_w)


def _moe_ffn(x, mod, nw, final_w, router_w, router_b, wg, wu, wd, final):
    b, t, d = x.shape
    n = b * t
    x2 = x.reshape(n, d)
    h, route, counts = _route(x2, mod, nw, router_w, router_b, t)
    mb = _pick(n, (1024, 512, 256, 128))
    cnt = counts[0, :N_EXPERTS].astype(jnp.int32)
    nblk_e = (cnt + mb - 1) // mb
    blk_end = jnp.cumsum(nblk_e)
    pad_start = (blk_end - nblk_e) * mb
    nblk = (2 * n) // mb + N_EXPERTS
    bi = jnp.arange(nblk, dtype=jnp.int32)
    total_blk = blk_end[-1]
    blk_v = (bi < total_blk).astype(jnp.int32)
    clamped = jnp.minimum(bi, total_blk - 1).astype(jnp.int32)
    blk_e = jnp.sum((clamped[:, None] >= blk_end[None, :]).astype(jnp.int32), axis=1)
    blk_e = jnp.minimum(blk_e, N_EXPERTS - 1).astype(jnp.int32)
    e12 = route[:, R_E1:R_E2 + 1].astype(jnp.int32)
    rank12 = route[:, R_RANK1:R_RANK2 + 1].astype(jnp.int32)
    dest = (pad_start[e12] + rank12).reshape(n * 2)
    xs = _dispatch(h, dest, nblk * mb)
    ys = _experts(xs, blk_e, clamped, blk_v, wg, wu, wd, mb)
    out = _combine(ys, dest, x2, route, mod, final_w, t, final)
    return out.reshape(b, t, d)


def _pad_in_weight(w):
    d = w.shape[0]
    ab_end = 4 * DN_DIM + 4 * DN_HEADS
    return jnp.concatenate([w[:, :ab_end], jnp.zeros((d, SEG_AB[1] - ab_end), w.dtype), w[:, ab_end:]],
                           axis=1).astype(BF16)


def kernel(x, c, ctx, c_ctx, w_mod, b_mod, norm1_w, norm2_w, w_in, dn_conv_w, dn_a_log, dn_dt_bias, dn_norm_w, at_qnorm_w, at_knorm_w, w_out, ffn_w_gate, ffn_w_up, ffn_w_down, moe_router_w, moe_router_b, moe_w_gate, moe_w_up, moe_w_down, final_norm_w):
    b, t, d = x.shape
    depth = w_mod.shape[0]
    xc = ctx
    rows = -(-(b + 1) // SUBLANES) * SUBLANES
    c_rows = jnp.zeros((rows, d), F32).at[:b].set(c).at[b].set(c_ctx)
    mod_all = _modulation(c_rows, w_mod, b_mod)
    final_w = final_norm_w.reshape(1, d)
    zero_state = jnp.zeros((b, 2, DN_GROUPS, MXU_DIM, MXU_DIM), F32)

    for layer in range(depth):
        last = layer == depth - 1
        mod = mod_all[layer, :b].reshape(b, 1, 6 * d)
        mod_c = jnp.broadcast_to(mod_all[layer, b].reshape(1, 1, 6 * d), (b, 1, 6 * d))
        n1 = norm1_w[layer].reshape(1, d)
        n2 = norm2_w[layer].reshape(1, d)
        w_pad = _pad_in_weight(w_in[layer])
        w_o = w_out[layer].astype(BF16)

        qkv, z, ab, atq, atk, atv = _in_projection(x, mod, n1, w_pad)
        qkv_c, z_c, ab_c, atq_c, atk_c, atv_c = _in_projection(xc, mod_c, n1, w_pad)

        prep_c = _dn_prepare(qkv_c, ab_c, dn_conv_w[layer], dn_a_log[layer], dn_dt_bias[layer])
        of_c, ob_c, s_ctx = _dn_scan(prep_c, zero_state)
        prep = _dn_prepare(qkv, ab, dn_conv_w[layer], dn_a_log[layer], dn_dt_bias[layer])
        o_f, o_b, _ = _dn_scan(prep, s_ctx)

        q_l, kd_l, vd_l = _attn_prepare(atq, atk, atv, at_qnorm_w[layer], at_knorm_w[layer], rope=True)
        q_c, kd_c, vd_c = _attn_prepare(atq_c, atk_c, atv_c, at_qnorm_w[layer], at_knorm_w[layer], rope=False)
        at = _flash_attention(q_l, jnp.concatenate([kd_l, kd_c], axis=2), jnp.concatenate([vd_l, vd_c], axis=3))

        x = _out_projection(o_f, o_b, z, at, x, mod, dn_norm_w[layer], w_o)
        if not last:
            at_c = _flash_attention(q_c, kd_c, vd_c)
            xc = _out_projection(of_c, ob_c, z_c, at_c, xc, mod_c, dn_norm_w[layer], w_o)

        if layer % 2 == 0:
            i = layer // 2
            wg, wu, wd = (w[i].astype(BF16) for w in (ffn_w_gate, ffn_w_up, ffn_w_down))
            x = _dense_ffn(x, mod, n2, final_w, wg, wu, wd, last)
            if not last:
                xc = _dense_ffn(xc, mod_c, n2, final_w, wg, wu, wd, False)
        else:
            j = layer // 2
            wg, wu, wd = (w[j].astype(BF16) for w in (moe_w_gate, moe_w_up, moe_w_down))
            x = _moe_ffn(x, mod, n2, final_w, moe_router_w[j], moe_router_b[j], wg, wu, wd, last)
            if not last:
                xc = _moe_ffn(xc, mod_c, n2, final_w, moe_router_w[j], moe_router_b[j], wg, wu, wd, False)
    return x
```

```python
import functools
import math

import jax
import jax.numpy as jnp
from jax import lax
from jax.experimental import pallas as pl
from jax.experimental.pallas import tpu as pltpu

F32 = jnp.float32
BF16 = jnp.bfloat16
EPS = 1e-6
LOG2E = math.log2(math.e)

DN_HEADS = 8
HEAD_DIM = 64
DN_DIM = DN_HEADS * HEAD_DIM
DN_CONV_W = 5
DN_CHUNK = 64
AT_Q_HEADS = 8
AT_KV_HEADS = 2
AT_GROUP = AT_Q_HEADS // AT_KV_HEADS
AT_Q_DIM = AT_Q_HEADS * HEAD_DIM
AT_KV_DIM = AT_KV_HEADS * HEAD_DIM
GRID_W = 64
ROPE_THETA = 10000.0
N_EXPERTS = 8

LANES = 128
SUBLANES = 8
MXU_DIM = 256
VMEM_LIMIT = 52 * 1024 * 1024

HEADS_PER_GROUP = MXU_DIM // HEAD_DIM
DN_GROUPS = DN_HEADS // HEADS_PER_GROUP
FLASH_ROWS = 256
FLASH_BATCH = 8
PV_ROWS = HEAD_DIM + 16
DMA_UNROLL = 8

SEG_QKV = (0, 3 * DN_DIM)
SEG_Z = (3 * DN_DIM, 4 * DN_DIM)
SEG_AB = (4 * DN_DIM, 4 * DN_DIM + LANES)
SEG_ATQ = (SEG_AB[1], SEG_AB[1] + AT_Q_DIM)
SEG_ATK = (SEG_ATQ[1], SEG_ATQ[1] + AT_KV_DIM)
SEG_ATV = (SEG_ATK[1], SEG_ATK[1] + AT_KV_DIM)
IN_PAD = SEG_ATV[1]
SEGS = (SEG_QKV, SEG_Z, SEG_AB, SEG_ATQ, SEG_ATK, SEG_ATV)


def _cparams(sem):
    return pltpu.CompilerParams(dimension_semantics=sem, vmem_limit_bytes=VMEM_LIMIT)


def _bdot(a, b):
    return jnp.dot(a.astype(BF16), b.astype(BF16), preferred_element_type=F32)


def _bdot_nt(a, b):
    return lax.dot_general(a.astype(BF16), b.astype(BF16), (((1,), (1,)), ((), ())),
                           preferred_element_type=F32)


def _bdot_tn(a, b):
    return lax.dot_general(a.astype(BF16), b.astype(BF16), (((0,), (0,)), ((), ())),
                           preferred_element_type=F32)


def _split3(x):
    hi = x.astype(BF16)
    r1 = x - hi.astype(F32)
    mid = r1.astype(BF16)
    lo = (r1 - mid.astype(F32)).astype(BF16)
    return hi, mid, lo


def _dot_sel(x, sel):
    dot = functools.partial(jnp.dot, preferred_element_type=F32)
    hi, mid, lo = _split3(x)
    return dot(hi, sel) + dot(mid, sel) + dot(lo, sel)


def _dot_sel_left(sel, x):
    dot = functools.partial(jnp.dot, preferred_element_type=F32)
    hi, mid, lo = _split3(x)
    return dot(sel, hi) + dot(sel, mid) + dot(sel, lo)


def _sigmoid(x):
    return 1.0 / (1.0 + jnp.exp(-x))


def _silu(x):
    return x * _sigmoid(x)


def _norm_modulate(x, nw, shift, scale):
    y = x * lax.rsqrt(jnp.mean(x * x, axis=-1, keepdims=True) + EPS)
    return (y * nw) * (1.0 + scale) + shift


def _pick(n, options):
    for o in options:
        if n % o == 0:
            return o
    raise ValueError(f"no tile in {options} divides {n}")


def _mod_kernel(c_ref, w_ref, b_ref, o_ref):
    o_ref[...] = _bdot(_silu(c_ref[...]), w_ref[...]) + b_ref[...]


def _modulation(c_rows, w_mod, b_mod):
    nl, d, n6 = w_mod.shape
    rows = c_rows.shape[0]
    tn = _pick(n6, (1536, 1024, 512, 256, 128))
    return pl.pallas_call(
        _mod_kernel,
        grid=(nl, n6 // tn),
        in_specs=[pl.BlockSpec((rows, d), lambda l, j: (0, 0)),
                  pl.BlockSpec((None, d, tn), lambda l, j: (l, 0, j)),
                  pl.BlockSpec((None, 1, tn), lambda l, j: (l, 0, j))],
        out_specs=pl.BlockSpec((None, rows, tn), lambda l, j: (l, 0, j)),
        out_shape=jax.ShapeDtypeStruct((nl, rows, n6), F32),
        compiler_params=_cparams(("parallel", "parallel")),
        name="adaln_modulation",
    )(c_rows, w_mod, b_mod.reshape(nl, 1, n6))


def _inproj_kernel(x_ref, mod_ref, nw_ref, w_ref, *out_refs, d):
    h = _norm_modulate(x_ref[...], nw_ref[...], mod_ref[:, 0:d], mod_ref[:, d:2 * d]).astype(BF16)
    for (a, b), o_ref in zip(SEGS, out_refs):
        o_ref[...] = jnp.dot(h, w_ref[:, a:b], preferred_element_type=F32)


def _in_projection(x, mod, nw, w_pad):
    b, t, d = x.shape
    tm = _pick(t, (512, 256, 128))
    return pl.pallas_call(
        functools.partial(_inproj_kernel, d=d),
        grid=(b, t // tm),
        in_specs=[pl.BlockSpec((None, tm, d), lambda i, j: (i, j, 0)),
                  pl.BlockSpec((None, 1, 6 * d), lambda i, j: (i, 0, 0)),
                  pl.BlockSpec((1, d), lambda i, j: (0, 0)),
                  pl.BlockSpec((d, IN_PAD), lambda i, j: (0, 0))],
        out_specs=[pl.BlockSpec((None, tm, s1 - s0), lambda i, j: (i, j, 0)) for s0, s1 in SEGS],
        out_shape=[jax.ShapeDtypeStruct((b, t, s1 - s0), F32) for s0, s1 in SEGS],
        compiler_params=_cparams(("parallel", "parallel")),
        name="norm_in_projection",
    )(x, mod, nw, w_pad)


def _dn_prep_kernel(xp_ref, x_ref, xn_ref, ab_ref, cw_ref, gp_ref, head_ones_ref, cum_f_ref, cum_b_ref,
                    expand_ref, q_ref, k_ref, v_ref, bf_ref, gf_ref, bb_ref, gb_ref, gc_ref, xe_ref, *, tm):
    i = pl.program_id(1)
    halo = SUBLANES
    pad = (DN_CONV_W - 1) // 2
    zeros = jnp.zeros((halo, 3 * DN_DIM), F32)
    xe_ref[0:halo, :] = jnp.where(i > 0, xp_ref[...], zeros)
    xe_ref[halo:halo + tm, :] = x_ref[...]
    xe_ref[halo + tm:halo + tm + halo, :] = jnp.where(i < pl.num_programs(1) - 1, xn_ref[...], zeros)
    acc = cw_ref[0:1, :] * xe_ref[halo - pad:halo - pad + tm, :]
    for j in range(1, DN_CONV_W):
        acc = acc + cw_ref[j:j + 1, :] * xe_ref[halo - pad + j:halo - pad + j + tm, :]
    qkv = _silu(acc)
    ones = head_ones_ref[...]
    q = qkv[:, 0:DN_DIM]
    k = qkv[:, DN_DIM:2 * DN_DIM]
    q_ref[...] = q * lax.rsqrt(_dot_sel(q * q, ones) + EPS) * (HEAD_DIM ** -0.5)
    k_ref[...] = k * lax.rsqrt(_dot_sel(k * k, ones) + EPS)
    v_ref[...] = qkv[:, 2 * DN_DIM:3 * DN_DIM]

    ab = ab_ref[...]
    lane = lax.broadcasted_iota(jnp.int32, ab.shape, 1)
    z = ab + gp_ref[1:2, :]
    softplus = jnp.maximum(z, 0.0) + jnp.log(1.0 + jnp.exp(-jnp.abs(z)))
    g = -jnp.exp(gp_ref[0:1, :]) * softplus
    g = jnp.where(lane < 2 * DN_HEADS, g, 0.0)
    gc = jnp.where(lane < DN_HEADS, _dot_sel_left(cum_f_ref[...], g), _dot_sel_left(cum_b_ref[...], g))
    gc_ref[...] = gc
    pieces = _split3(jnp.where(lane < 2 * DN_HEADS, gc, _sigmoid(ab)))
    for n, o_ref in enumerate((gf_ref, gb_ref, bf_ref, bb_ref)):
        e = expand_ref[n]
        o_ref[...] = sum(jnp.dot(p, e, preferred_element_type=F32) for p in pieces)


def _head_ones(width):
    idx = jnp.arange(width) // HEAD_DIM
    return (idx[:, None] == idx[None, :]).astype(BF16)


def _dn_prepare(qkv, ab, conv_w, a_log, dt_bias):
    b, t, c = qkv.shape
    tm = _pick(t, (256, 128, 64))
    nt = t // tm
    hb = tm // SUBLANES
    last8 = t // SUBLANES - 1
    gp = jnp.zeros((SUBLANES, LANES), F32)
    gp = gp.at[0, :2 * DN_HEADS].set(a_log.reshape(-1)).at[1, :2 * DN_HEADS].set(dt_bias.reshape(-1))
    r = jnp.arange(tm)
    same = (r[:, None] // DN_CHUNK) == (r[None, :] // DN_CHUNK)
    cum_f = (same & (r[None, :] <= r[:, None])).astype(BF16)
    cum_b = (same & (r[None, :] >= r[:, None])).astype(BF16)
    col_head = jnp.arange(DN_DIM) // HEAD_DIM
    src = jnp.arange(LANES)[:, None]
    expand = jnp.stack([(src == col_head[None, :] + off) for off in
                        (0, DN_HEADS, 2 * DN_HEADS, 3 * DN_HEADS)]).astype(BF16)
    wide = pl.BlockSpec((None, tm, DN_DIM), lambda i, j: (i, j, 0))
    const = lambda shape: pl.BlockSpec(shape, lambda i, j: (0,) * len(shape))
    outs = pl.pallas_call(
        functools.partial(_dn_prep_kernel, tm=tm),
        grid=(b, nt),
        in_specs=[pl.BlockSpec((None, SUBLANES, c), lambda i, j: (i, jnp.maximum(j * hb - 1, 0), 0)),
                  pl.BlockSpec((None, tm, c), lambda i, j: (i, j, 0)),
                  pl.BlockSpec((None, SUBLANES, c), lambda i, j: (i, jnp.minimum((j + 1) * hb, last8), 0)),
                  pl.BlockSpec((None, tm, LANES), lambda i, j: (i, j, 0)),
                  const((DN_CONV_W, c)), const((SUBLANES, LANES)), const((DN_DIM, DN_DIM)),
                  const((tm, tm)), const((tm, tm)), const((4, LANES, DN_DIM))],
        out_specs=[wide] * 7 + [pl.BlockSpec((None, tm, LANES), lambda i, j: (i, j, 0))],
        out_shape=[jax.ShapeDtypeStruct((b, t, DN_DIM), F32)] * 7 + [jax.ShapeDtypeStruct((b, t, LANES), F32)],
        scratch_shapes=[pltpu.VMEM((tm + 2 * SUBLANES, c), F32)],
        compiler_params=_cparams(("parallel", "parallel")),
        name="deltanet_prepare",
    )(qkv, qkv, qkv, ab, conv_w, gp, _head_ones(DN_DIM), cum_f, cum_b, expand)
    return outs


def _tile4(x):
    return jnp.concatenate([x] * HEADS_PER_GROUP, axis=0)


def _group_masks():
    n = HEADS_PER_GROUP * DN_CHUNK
    row = lax.broadcasted_iota(jnp.int32, (n, n), 0)
    col = lax.broadcasted_iota(jnp.int32, (n, n), 1)
    return row, col, (row // DN_CHUNK) == (col // DN_CHUNK)


def _collapse(full, same_head):
    kept = jnp.where(same_head, full, 0.0)
    out = kept[0:DN_CHUNK]
    for h in range(1, HEADS_PER_GROUP):
        out = out + kept[h * DN_CHUNK:(h + 1) * DN_CHUNK]
    return out


def _dn_local_kernel(q_ref, k_ref, v_ref, btf_ref, gxf_ref, btb_ref, gxb_ref, gc_ref, gr_ref,
                     uf_ref, wf_ref, af_ref, ub_ref, wb_ref, ab_ref, *, chunks):
    c = DN_CHUNK
    n = HEADS_PER_GROUP * c
    row, col, same_head = _group_masks()
    ri, cj = row % c, col % c
    eye = (row == col).astype(F32)
    tri = {False: (same_head & (ri >= cj), same_head & (ri > cj)),
           True: (same_head & (ri <= cj), same_head & (ri < cj))}

    def joins(s):
        return ((row // (2 * s)) == (col // (2 * s))) & ((row // s) != (col // s))

    chains = []
    for cc in range(chunks):
        rs = slice(cc * c, (cc + 1) * c)
        for d, (bt_ref, gx_ref, outs) in enumerate(((btf_ref, gxf_ref, (uf_ref, wf_ref, af_ref)),
                                                    (btb_ref, gxb_ref, (ub_ref, wb_ref, ab_ref)))):
            for g in range(DN_GROUPS):
                chains.append((rs, slice(g * MXU_DIM, (g + 1) * MXU_DIM), d, g, cc, bt_ref, gx_ref, outs))

    a_mats, rhs_list = [], []
    for rs, sl, d, g, cc, bt_ref, gx_ref, outs in chains:
        incl, strict = tri[d == 1]
        q, k, v = q_ref[rs, sl], k_ref[rs, sl], v_ref[rs, sl]
        beta, gcx = bt_ref[rs, sl], gx_ref[rs, sl]
        lane0 = d * DN_HEADS + g * HEADS_PER_GROUP
        col_b = jnp.concatenate([jnp.broadcast_to(gc_ref[rs, lane0 + h:lane0 + h + 1], (c, n))
                                 for h in range(HEADS_PER_GROUP)], axis=0)
        diff = col_b - jnp.broadcast_to(gr_ref[d, cc, :, sl], (n, n))
        decay = jnp.where(incl, jnp.exp(jnp.where(incl, diff, 0.0)), 0.0)
        k_beta = k * beta
        k_t = _tile4(k)
        a_mats.append(jnp.where(strict, _bdot_nt(jnp.where(same_head, _tile4(k_beta), 0.0), k_t) * decay, 0.0))
        attn = _bdot_nt(jnp.where(same_head, _tile4(q), 0.0), k_t) * decay
        outs[2][rs, sl] = _collapse(attn, same_head)
        rhs_list.append(jnp.concatenate([_tile4(v * beta), _tile4(k_beta * jnp.exp(gcx))], axis=1))

    def half_rows(x, s, second):
        o = s if second else 0
        return jnp.concatenate([x[r0 + o:r0 + o + s] for r0 in range(0, n, 2 * s)], axis=0)

    def merge_rows(x, new_half, s, second):
        parts = []
        for bi, r0 in enumerate(range(0, n, 2 * s)):
            new = new_half[bi * s:(bi + 1) * s]
            parts += [x[r0:r0 + s], new] if second else [new, x[r0 + s:r0 + 2 * s]]
        return jnp.concatenate(parts, axis=0)

    second = [d == 0 for _, _, d, _, _, _, _, _ in chains]
    invs = [eye - jnp.where(joins(1), a, 0.0) for a in a_mats]
    s = 2
    while s < c:
        js = joins(s)
        if s < SUBLANES:
            mids = [_bdot(jnp.where(js, a, 0.0), inv) for a, inv in zip(a_mats, invs)]
            invs = [inv - _bdot(inv, mid) for inv, mid in zip(invs, mids)]
        else:
            zeros = jnp.zeros((n, n), F32)
            mids = [merge_rows(zeros, _bdot(half_rows(jnp.where(js, a, 0.0), s, sec), inv), s, sec)
                    for a, inv, sec in zip(a_mats, invs, second)]
            invs = [merge_rows(inv, half_rows(inv, s, sec) - _bdot(half_rows(inv, s, sec), mid), s, sec)
                    for inv, mid, sec in zip(invs, mids, second)]
        s *= 2

    for (rs, sl, d, g, cc, bt_ref, gx_ref, outs), inv, rhs in zip(chains, invs, rhs_list):
        sol = _bdot(inv, rhs)
        outs[0][rs, sl] = _collapse(sol[:, 0:n], same_head)
        outs[1][rs, sl] = _collapse(sol[:, n:2 * n], same_head)


def _dn_local(prep):
    q, k, v, beta_f, gcx_f, beta_b, gcx_b, gc = prep
    b, t, _ = q.shape
    n = t // DN_CHUNK
    chunks = 4 if n % 4 == 0 else (2 if n % 2 == 0 else 1)
    rows = chunks * DN_CHUNK
    g_rows = gc[:, :, :2 * DN_HEADS].reshape(b, n, DN_CHUNK, 2, DN_HEADS)
    g_rows = g_rows.transpose(0, 3, 1, 4, 2).reshape(b, 2, n, 1, DN_DIM)
    wide = pl.BlockSpec((None, rows, DN_DIM), lambda i, j: (i, j, 0))
    return pl.pallas_call(
        functools.partial(_dn_local_kernel, chunks=chunks),
        grid=(b, n // chunks),
        in_specs=[wide] * 7 + [pl.BlockSpec((None, rows, LANES), lambda i, j: (i, j, 0)),
                               pl.BlockSpec((None, 2, chunks, 1, DN_DIM), lambda i, j: (i, 0, j, 0, 0))],
        out_specs=[wide] * 6,
        out_shape=[jax.ShapeDtypeStruct((b, t, DN_DIM), F32)] * 6,
        compiler_params=_cparams(("parallel", "parallel")),
        name="deltanet_chunk_local",
    )(q, k, v, beta_f, gcx_f, beta_b, gcx_b, gc, g_rows)


def _dn_scan_kernel(qf, kf, gxf, uf, wf, af, qb, kb, gxb, ub, wb, ab, s0_ref, of_ref, ob_ref, sfin_ref, s_ref,
                    *, chunks):
    step = pl.program_id(1)
    c = DN_CHUNK

    @pl.when(step == 0)
    def _():
        s_ref[...] = s0_ref[...]

    _, _, same_head = _group_masks()
    dirs = ((qf, kf, gxf, uf, wf, af, of_ref), (qb, kb, gxb, ub, wb, ab, ob_ref))
    states = {(d, g): s_ref[d, g] for d in range(2) for g in range(DN_GROUPS)}
    for cc in range(chunks):
        for d, (q, k, gx, u, w, at, o_ref) in enumerate(dirs):
            reverse = d == 1
            ck = chunks - 1 - cc if reverse else cc
            rs = slice(ck * c, (ck + 1) * c)
            for g in range(DN_GROUPS):
                sl = slice(g * MXU_DIM, (g + 1) * MXU_DIM)
                state = states[(d, g)]
                gcx = gx[rs, sl]
                g_last = gcx[0:1, :] if reverse else gcx[c - 1:c, :]
                q_dec = q[rs, sl] * jnp.exp(gcx)
                k_dec = k[rs, sl] * jnp.exp(g_last - gcx)
                ws = _bdot(jnp.concatenate([w[rs, sl], q_dec], axis=0), state)
                v_new = u[rs, sl] - ws[0:c]
                attn = jnp.where(same_head, _tile4(at[rs, sl]), 0.0)
                o_ref[rs, sl] = ws[c:2 * c] + _collapse(_bdot(attn, _tile4(v_new)), same_head)
                states[(d, g)] = state * jnp.exp(g_last) + jnp.where(same_head, _bdot_tn(k_dec, v_new), 0.0)
    for (d, g), state in states.items():
        s_ref[d, g] = state

    @pl.when(step == pl.num_programs(1) - 1)
    def _():
        sfin_ref[...] = s_ref[...]


def _dn_scan(prep, s0):
    q, k, _, _, gcx_f, _, gcx_b, _ = prep
    u_f, w_f, a_f, u_b, w_b, a_b = _dn_local(prep)
    b, t, _ = q.shape
    n = t // DN_CHUNK
    chunks = 4 if n % 4 == 0 else (2 if n % 2 == 0 else 1)
    rows = chunks * DN_CHUNK
    steps = n // chunks
    fwd = pl.BlockSpec((None, rows, DN_DIM), lambda i, j: (i, j, 0))
    bwd = pl.BlockSpec((None, rows, DN_DIM), lambda i, j: (i, steps - 1 - j, 0))
    state = pl.BlockSpec((None, 2, DN_GROUPS, MXU_DIM, MXU_DIM), lambda i, j: (i, 0, 0, 0, 0))
    o_f, o_b, s_fin = pl.pallas_call(
        functools.partial(_dn_scan_kernel, chunks=chunks),
        grid=(b, steps),
        in_specs=[fwd] * 6 + [bwd] * 6 + [state],
        out_specs=[fwd, bwd, state],
        out_shape=[jax.ShapeDtypeStruct((b, t, DN_DIM), F32)] * 2
        + [jax.ShapeDtypeStruct((b, 2, DN_GROUPS, MXU_DIM, MXU_DIM), F32)],
        scratch_shapes=[pltpu.VMEM((2, DN_GROUPS, MXU_DIM, MXU_DIM), F32)],
        compiler_params=_cparams(("parallel", "arbitrary")),
        name="deltanet_scan",
    )(q, k, gcx_f, u_f, w_f, a_f, q, k, gcx_b, u_b, w_b, a_b, s0)
    return o_f, o_b, s_fin


def _rope(x, cos_t, sin_t):
    width = x.shape[-1]
    half = HEAD_DIM // 2
    lane = lax.broadcasted_iota(jnp.int32, x.shape, 1)
    partner = jnp.where(lane % HEAD_DIM < half, pltpu.roll(x, width - half, 1), pltpu.roll(x, half, 1))
    reps = width // LANES
    cos_w = jnp.concatenate([cos_t] * reps, axis=1) if reps > 1 else cos_t
    sin_w = jnp.concatenate([sin_t] * reps, axis=1) if reps > 1 else sin_t
    return x * cos_w + partner * sin_w


def _head_rmsnorm(x, ones, w):
    ms = _dot_sel(x * x, ones) * (1.0 / HEAD_DIM)
    return x * lax.rsqrt(ms + EPS) * w


def _attn_prep_kernel(q_ref, k_ref, v_ref, qw_ref, kw_ref, cos_ref, sin_ref, ones_q_ref, ones_k_ref,
                      qo_ref, ko_ref, vo_ref, *, rope):
    q = _head_rmsnorm(q_ref[...], ones_q_ref[...], qw_ref[...])
    k = _head_rmsnorm(k_ref[...], ones_k_ref[...], kw_ref[...])
    if rope:
        q = _rope(q, cos_ref[...], sin_ref[...])
        k = _rope(k, cos_ref[...], sin_ref[...])
    qo_ref[...] = (q * (HEAD_DIM ** -0.5 * LOG2E)).astype(BF16)
    v = v_ref[...]
    lane = lax.broadcasted_iota(jnp.int32, k.shape, 1)
    first = lane < HEAD_DIM
    k_sw = pltpu.roll(k, HEAD_DIM, 1)
    v_sw = pltpu.roll(v, HEAD_DIM, 1)
    ko_ref[0] = jnp.where(first, k, k_sw).astype(BF16)
    ko_ref[1] = jnp.where(first, k_sw, k).astype(BF16)
    vo_ref[0] = jnp.where(first, v, 1.0).T[0:PV_ROWS, :].astype(BF16)
    vo_ref[1] = jnp.where(first, v_sw, 1.0).T[0:PV_ROWS, :].astype(BF16)


def _rope_tables(t):
    rows = t // GRID_W
    row_pos = jnp.repeat(jnp.arange(rows, dtype=F32), GRID_W, total_repeat_length=t)
    col_pos = jnp.tile(jnp.arange(GRID_W, dtype=F32), rows)
    n_freq = HEAD_DIM // 4
    freqs = ROPE_THETA ** (-jnp.arange(n_freq, dtype=F32) / n_freq)
    ang = jnp.concatenate([row_pos[:, None] * freqs, col_pos[:, None] * freqs], axis=-1)
    cos, sin = jnp.cos(ang), jnp.sin(ang)
    reps = LANES // HEAD_DIM
    return (jnp.tile(jnp.concatenate([cos, cos], axis=-1), (1, reps)),
            jnp.tile(jnp.concatenate([-sin, sin], axis=-1), (1, reps)))


def _attn_prepare(atq, atk, atv, qn_w, kn_w, rope):
    b, t, _ = atq.shape
    tm = _pick(t, (512, 256, 128))
    if rope:
        cos_t, sin_t = _rope_tables(t)
    else:
        cos_t = sin_t = jnp.zeros((t, LANES), F32)
    qw = jnp.tile(qn_w, AT_Q_HEADS).reshape(1, AT_Q_DIM)
    kw = jnp.tile(kn_w, AT_KV_HEADS).reshape(1, AT_KV_DIM)
    const = lambda shape: pl.BlockSpec(shape, lambda i, j: (0,) * len(shape))
    tab = pl.BlockSpec((tm, LANES), lambda i, j: (j, 0))
    kv_out = pl.BlockSpec((None, AT_KV_HEADS, tm, AT_KV_DIM), lambda i, j: (i, 0, j, 0))
    return pl.pallas_call(
        functools.partial(_attn_prep_kernel, rope=rope),
        grid=(b, t // tm),
        in_specs=[pl.BlockSpec((None, tm, AT_Q_DIM), lambda i, j: (i, j, 0)),
                  pl.BlockSpec((None, tm, AT_KV_DIM), lambda i, j: (i, j, 0)),
                  pl.BlockSpec((None, tm, AT_KV_DIM), lambda i, j: (i, j, 0)),
                  const((1, AT_Q_DIM)), const((1, AT_KV_DIM)), tab, tab,
                  const((AT_Q_DIM, AT_Q_DIM)), const((AT_KV_DIM, AT_KV_DIM))],
        out_specs=[pl.BlockSpec((None, tm, AT_Q_DIM), lambda i, j: (i, j, 0)), kv_out,
                   pl.BlockSpec((None, AT_KV_HEADS, PV_ROWS, tm), lambda i, j: (i, 0, 0, j))],
        out_shape=[jax.ShapeDtypeStruct((b, t, AT_Q_DIM), BF16),
                   jax.ShapeDtypeStruct((b, AT_KV_HEADS, t, AT_KV_DIM), BF16),
                   jax.ShapeDtypeStruct((b, AT_KV_HEADS, PV_ROWS, t), BF16)],
        compiler_params=_cparams(("parallel", "parallel")),
        name="attention_prepare",
    )(atq, atk, atv, qw, kw, cos_t, sin_t, _head_ones(AT_Q_DIM), _head_ones(AT_KV_DIM))


def _flash_kernel(q_ref, k_ref, vt_ref, o_ref, qs_ref, m_ref, acc_ref, *, tq):
    j = pl.program_id(3)

    @pl.when(j == 0)
    def _():
        q = q_ref[...]
        lane = lax.broadcasted_iota(jnp.int32, q.shape, 1)
        for h in range(AT_GROUP):
            qs_ref[h * tq:(h + 1) * tq, :] = jnp.where(lane // HEAD_DIM == h, q, jnp.zeros_like(q))
        m_ref[...] = jnp.full(m_ref.shape, -jnp.inf, F32)
        acc_ref[...] = jnp.zeros(acc_ref.shape, F32)

    k = k_ref[...]
    k2 = jnp.concatenate([k, k], axis=1)
    vt = vt_ref[...]
    all_blocks = [slice(r0, r0 + FLASH_ROWS) for r0 in range(0, AT_GROUP * tq, FLASH_ROWS)]
    for b0 in range(0, len(all_blocks), FLASH_BATCH):
        blocks = all_blocks[b0:b0 + FLASH_BATCH]
        scores = [lax.dot_general(k2, qs_ref[cols, :], (((1,), (1,)), ((), ())), preferred_element_type=F32)
                  for cols in blocks]
        m_prev = [m_ref[:, cols] for cols in blocks]
        acc_prev = [acc_ref[:, cols] for cols in blocks]
        m_next = [jnp.maximum(mp, jnp.max(s, axis=0, keepdims=True)) for mp, s in zip(m_prev, scores)]
        probs = [jnp.exp2(s - mn[0:1, :]).astype(BF16) for s, mn in zip(scores, m_next)]
        pv = [jnp.dot(vt, p, preferred_element_type=F32) for p in probs]
        for cols, mp, mn, ap, x in zip(blocks, m_prev, m_next, acc_prev, pv):
            acc_ref[:, cols] = jnp.exp2(mp[0:1, :] - mn[0:1, :]) * ap + x
            m_ref[:, cols] = mn

    @pl.when(j == pl.num_programs(3) - 1)
    def _():
        def normalised(h):
            a = acc_ref[:, h * tq:(h + 1) * tq]
            return a[0:HEAD_DIM, :] / a[HEAD_DIM:HEAD_DIM + 1, :]

        pairs = [jnp.concatenate([normalised(2 * p2), normalised(2 * p2 + 1)], axis=0).T
                 for p2 in range(AT_GROUP // 2)]
        o_ref[...] = jnp.concatenate(pairs, axis=1)


def _flash_attention(q, kd, vd):
    b, t, _ = q.shape
    s = kd.shape[2]
    tq = _pick(t, (1024, 512, 256, 128))
    tk = _pick(s, (1408, 768, 512, 256, 128))
    gw = AT_GROUP * HEAD_DIM
    return pl.pallas_call(
        functools.partial(_flash_kernel, tq=tq),
        grid=(b, AT_KV_HEADS, t // tq, s // tk),
        in_specs=[pl.BlockSpec((None, tq, gw), lambda i, g, a, j: (i, a, g)),
                  pl.BlockSpec((None, None, tk, AT_KV_DIM), lambda i, g, a, j: (i, g, j, 0)),
                  pl.BlockSpec((None, None, PV_ROWS, tk), lambda i, g, a, j: (i, g, 0, j))],
        out_specs=pl.BlockSpec((None, tq, gw), lambda i, g, a, j: (i, a, g)),
        out_shape=jax.ShapeDtypeStruct((b, t, AT_Q_DIM), F32),
        scratch_shapes=[pltpu.VMEM((AT_GROUP * tq, gw), BF16),
                        pltpu.VMEM((SUBLANES, AT_GROUP * tq), F32),
                        pltpu.VMEM((PV_ROWS, AT_GROUP * tq), F32)],
        compiler_params=_cparams(("parallel", "parallel", "parallel", "arbitrary")),
        name="gqa_flash_attention",
    )(q, kd, vd)


def _outproj_kernel(of_ref, ob_ref, z_ref, at_ref, x_ref, mod_ref, nw_ref, ones_ref, w_ref, o_ref, *, d):
    o = of_ref[...] + ob_ref[...]
    ms = _dot_sel(o * o, ones_ref[...]) * (1.0 / HEAD_DIM)
    dn = o * lax.rsqrt(ms + EPS) * nw_ref[...] * _silu(z_ref[...])
    mix = _bdot(dn, w_ref[0:DN_DIM, :]) + _bdot(at_ref[...], w_ref[DN_DIM:DN_DIM + AT_Q_DIM, :])
    o_ref[...] = x_ref[...] + mod_ref[:, 2 * d:3 * d] * mix


def _out_projection(o_f, o_b, z, at, x, mod, dn_norm_w, w_out):
    b, t, d = x.shape
    tm = _pick(t, (512, 256, 128))
    half = pl.BlockSpec((None, tm, DN_DIM), lambda i, j: (i, j, 0))
    full = pl.BlockSpec((None, tm, d), lambda i, j: (i, j, 0))
    const = lambda shape: pl.BlockSpec(shape, lambda i, j: (0,) * len(shape))
    return pl.pallas_call(
        functools.partial(_outproj_kernel, d=d),
        grid=(b, t // tm),
        in_specs=[half, half, half, half, full, pl.BlockSpec((None, 1, 6 * d), lambda i, j: (i, 0, 0)),
                  const((1, DN_DIM)), const((DN_DIM, DN_DIM)), const((DN_DIM + AT_Q_DIM, d))],
        out_specs=full,
        out_shape=jax.ShapeDtypeStruct((b, t, d), F32),
        compiler_params=_cparams(("parallel", "parallel")),
        name="mix_out_projection",
    )(o_f, o_b, z, at, x, mod, jnp.tile(dn_norm_w, DN_HEADS).reshape(1, DN_DIM), _head_ones(DN_DIM), w_out)


def _ffn_kernel(x_ref, mod_ref, nw_ref, fw_ref, wg_ref, wu_ref, wd_ref, o_ref, h_ref, acc_ref, *, d, final):
    f = pl.program_id(2)

    @pl.when(f == 0)
    def _():
        h_ref[...] = _norm_modulate(x_ref[...], nw_ref[...], mod_ref[:, 3 * d:4 * d],
                                    mod_ref[:, 4 * d:5 * d]).astype(BF16)
        acc_ref[...] = jnp.zeros(acc_ref.shape, F32)

    h = h_ref[...]
    gate = jnp.dot(h, wg_ref[...], preferred_element_type=F32)
    up = jnp.dot(h, wu_ref[...], preferred_element_type=F32)
    acc_ref[...] += jnp.dot((_silu(gate) * up).astype(BF16), wd_ref[...], preferred_element_type=F32)

    @pl.when(f == pl.num_programs(2) - 1)
    def _():
        y = x_ref[...] + mod_ref[:, 5 * d:6 * d] * acc_ref[...]
        if final:
            y = y * lax.rsqrt(jnp.mean(y * y, axis=-1, keepdims=True) + EPS) * fw_ref[...]
        o_ref[...] = y


def _dense_ffn(x, mod, nw, final_w, wg, wu, wd, final):
    b, t, d = x.shape
    fdim = wg.shape[1]
    tm = _pick(t, (1024, 512, 256, 128))
    tf = _pick(fdim, (512, 256, 128))
    full = pl.BlockSpec((None, tm, d), lambda i, j, f: (i, j, 0))
    const = lambda shape: pl.BlockSpec(shape, lambda i, j, f: (0,) * len(shape))
    return pl.pallas_call(
        functools.partial(_ffn_kernel, d=d, final=final),
        grid=(b, t // tm, fdim // tf),
        in_specs=[full, pl.BlockSpec((None, 1, 6 * d), lambda i, j, f: (i, 0, 0)), const((1, d)), const((1, d)),
                  pl.BlockSpec((d, tf), lambda i, j, f: (0, f)),
                  pl.BlockSpec((d, tf), lambda i, j, f: (0, f)),
                  pl.BlockSpec((tf, d), lambda i, j, f: (f, 0))],
        out_specs=full,
        out_shape=jax.ShapeDtypeStruct((b, t, d), F32),
        scratch_shapes=[pltpu.VMEM((tm, d), BF16), pltpu.VMEM((tm, d), F32)],
        compiler_params=_cparams(("parallel", "parallel", "arbitrary")),
        name="dense_swiglu",
    )(x, mod, nw, final_w, wg, wu, wd)


R_E1, R_E2, R_W1, R_W2, R_RANK1, R_RANK2 = range(6)


def _router_kernel(x_ref, mod_ref, nw_ref, rw_ref, rb_ref, tri_ref, h_ref, route_ref, cnt_ref, carry_ref, *, d):
    @pl.when(pl.program_id(0) == 0)
    def _():
        carry_ref[...] = jnp.zeros(carry_ref.shape, F32)

    h = _norm_modulate(x_ref[...], nw_ref[...], mod_ref[:, 3 * d:4 * d], mod_ref[:, 4 * d:5 * d])
    h_ref[...] = h
    logits = _bdot(h, rw_ref[...]) + rb_ref[...]
    lane = lax.broadcasted_iota(jnp.int32, logits.shape, 1).astype(F32)
    neg = jnp.float32(-jnp.inf)
    lg = jnp.where(lane < N_EXPERTS, logits, neg)
    m1 = jnp.max(lg, axis=1, keepdims=True)
    i1 = jnp.min(jnp.where(lg == m1, lane, float(LANES)), axis=1, keepdims=True)
    hot1 = lane == i1
    lg2 = jnp.where(hot1, neg, lg)
    m2 = jnp.max(lg2, axis=1, keepdims=True)
    i2 = jnp.min(jnp.where(lg2 == m2, lane, float(LANES)), axis=1, keepdims=True)
    hot2 = lane == i2
    e2 = jnp.exp(m2 - m1)
    w1 = 1.0 / (1.0 + e2)
    w2 = e2 / (1.0 + e2)
    member = jnp.where(hot1 | hot2, 1.0, 0.0)
    before = jnp.dot(tri_ref[...], member.astype(BF16), preferred_element_type=F32) + carry_ref[0:1, :]
    rank1 = jnp.sum(jnp.where(hot1, before, 0.0), axis=1, keepdims=True)
    rank2 = jnp.sum(jnp.where(hot2, before, 0.0), axis=1, keepdims=True)
    rec = jnp.zeros(logits.shape, F32)
    for ln, val in ((R_E1, i1), (R_E2, i2), (R_W1, w1), (R_W2, w2), (R_RANK1, rank1), (R_RANK2, rank2)):
        rec = jnp.where(lane == float(ln), val, rec)
    route_ref[...] = rec
    tb = member.shape[0]
    total = before[tb - 1:tb, :] + member[tb - 1:tb, :]
    carry_ref[...] = jnp.broadcast_to(total, carry_ref.shape)
    cnt_ref[...] = jnp.broadcast_to(total, cnt_ref.shape)


def _route(x2, mod, nw, router_w, router_b, rows_per_mod):
    n, d = x2.shape
    tb = _pick(rows_per_mod, (512, 256, 128))
    per = rows_per_mod // tb
    rw = jnp.zeros((d, LANES), BF16).at[:, :N_EXPERTS].set(router_w.astype(BF16))
    rb = jnp.zeros((1, LANES), F32).at[0, :N_EXPERTS].set(router_b)
    r = jnp.arange(tb)
    tri = (r[None, :] < r[:, None]).astype(BF16)
    const = lambda shape: pl.BlockSpec(shape, lambda i: (0,) * len(shape))
    return pl.pallas_call(
        functools.partial(_router_kernel, d=d),
        grid=(n // tb,),
        in_specs=[pl.BlockSpec((tb, d), lambda i: (i, 0)),
                  pl.BlockSpec((None, 1, 6 * d), lambda i: (i // per, 0, 0)),
                  const((1, d)), const((d, LANES)), const((1, LANES)), const((tb, tb))],
        out_specs=[pl.BlockSpec((tb, d), lambda i: (i, 0)), pl.BlockSpec((tb, LANES), lambda i: (i, 0)),
                   const((SUBLANES, LANES))],
        out_shape=[jax.ShapeDtypeStruct((n, d), F32), jax.ShapeDtypeStruct((n, LANES), F32),
                   jax.ShapeDtypeStruct((SUBLANES, LANES), F32)],
        scratch_shapes=[pltpu.VMEM((SUBLANES, LANES), F32)],
        compiler_params=_cparams(("arbitrary",)),
        name="moe_router_top2",
    )(x2, mod, nw, rw, rb, tri)


def _dispatch_kernel(dest_ref, h_ref, xs_in_ref, xs_ref, sem, *, tb):
    del xs_in_ref

    def copy(t, kk):
        return pltpu.make_async_copy(h_ref.at[pl.ds(t, 1), :],
                                     xs_ref.at[pl.ds(dest_ref[0, 2 * t + kk], 1), :], sem)

    def start(t, c):
        copy(t, 0).start(priority=0)
        copy(t, 1).start(priority=1)
        return c

    def wait(t, c):
        copy(t, 0).wait()
        copy(t, 1).wait()
        return c

    lax.fori_loop(0, tb, start, 0, unroll=DMA_UNROLL)
    lax.fori_loop(0, tb, wait, 0, unroll=DMA_UNROLL)


def _dispatch(h, dest, p_rows):
    n, d = h.shape
    tb = _pick(n, (256, 128))
    return pl.pallas_call(
        functools.partial(_dispatch_kernel, tb=tb),
        grid=(n // tb,),
        in_specs=[pl.BlockSpec((None, 1, 2 * tb), lambda i: (i, 0, 0), memory_space=pltpu.SMEM),
                  pl.BlockSpec((tb, d), lambda i: (i, 0)), pl.BlockSpec(memory_space=pl.ANY)],
        out_specs=pl.BlockSpec(memory_space=pl.ANY),
        out_shape=jax.ShapeDtypeStruct((p_rows, d), F32),
        scratch_shapes=[pltpu.SemaphoreType.DMA(())],
        input_output_aliases={2: 0},
        compiler_params=_cparams(("arbitrary",)),
        name="moe_dispatch_rows",
    )(dest.reshape(n // tb, 1, 2 * tb), h, jnp.zeros((p_rows, d), F32))


def _expert_kernel(be_ref, bx_ref, bv_ref, x_ref, wg_ref, wu_ref, wd_ref, o_ref, h_ref, acc_ref):
    i, f = pl.program_id(0), pl.program_id(1)
    valid = bv_ref[i] == 1
    last = f == pl.num_programs(1) - 1

    @pl.when(valid & (f == 0))
    def _():
        h_ref[...] = x_ref[...].astype(BF16)
        acc_ref[...] = jnp.zeros(acc_ref.shape, F32)

    @pl.when(valid)
    def _():
        h = h_ref[...]
        gate = jnp.dot(h, wg_ref[...], preferred_element_type=F32)
        up = jnp.dot(h, wu_ref[...], preferred_element_type=F32)
        acc_ref[...] += jnp.dot((_silu(gate) * up).astype(BF16), wd_ref[...], preferred_element_type=F32)

    @pl.when(valid & last)
    def _():
        o_ref[...] = acc_ref[...]

    @pl.when(jnp.logical_not(valid) & last)
    def _():
        o_ref[...] = jnp.zeros(o_ref.shape, F32)


def _experts(xs, blk_e, blk_x, blk_v, wg, wu, wd, mb):
    p_rows, d = xs.shape
    fdim = wg.shape[2]
    tf = _pick(fdim, (512, 256, 128))
    nf = fdim // tf
    fsel = lambda i, f, bv: jnp.where(bv[i] == 1, f, nf - 1)
    grid_spec = pltpu.PrefetchScalarGridSpec(
        num_scalar_prefetch=3,
        grid=(p_rows // mb, nf),
        in_specs=[pl.BlockSpec((mb, d), lambda i, f, be, bx, bv: (bx[i], 0)),
                  pl.BlockSpec((None, d, tf), lambda i, f, be, bx, bv: (be[i], 0, fsel(i, f, bv))),
                  pl.BlockSpec((None, d, tf), lambda i, f, be, bx, bv: (be[i], 0, fsel(i, f, bv))),
                  pl.BlockSpec((None, tf, d), lambda i, f, be, bx, bv: (be[i], fsel(i, f, bv), 0))],
        out_specs=pl.BlockSpec((mb, d), lambda i, f, be, bx, bv: (i, 0)),
        scratch_shapes=[pltpu.VMEM((mb, d), BF16), pltpu.VMEM((mb, d), F32)])
    return pl.pallas_call(
        _expert_kernel,
        grid_spec=grid_spec,
        out_shape=jax.ShapeDtypeStruct((p_rows, d), F32),
        compiler_params=_cparams(("arbitrary", "arbitrary")),
        name="moe_expert_swiglu",
    )(blk_e, blk_x, blk_v, xs, wg, wu, wd)


def _combine_kernel(dest_ref, ys_ref, x_ref, route_ref, mod_ref, fw_ref, o_ref, y1_ref, y2_ref, sem, *,
                    d, tb, final):
    def copy(t, kk, buf):
        return pltpu.make_async_copy(ys_ref.at[pl.ds(dest_ref[0, 2 * t + kk], 1), :],
                                     buf.at[pl.ds(t, 1), :], sem)

    def start(t, c):
        copy(t, 0, y1_ref).start(priority=0)
        copy(t, 1, y2_ref).start(priority=1)
        return c

    def wait(t, c):
        copy(t, 0, y1_ref).wait()
        copy(t, 1, y2_ref).wait()
        return c

    lax.fori_loop(0, tb, start, 0, unroll=DMA_UNROLL)
    lax.fori_loop(0, tb, wait, 0, unroll=DMA_UNROLL)
    w1 = route_ref[:, R_W1:R_W1 + 1]
    w2 = route_ref[:, R_W2:R_W2 + 1]
    y = x_ref[...] + mod_ref[:, 5 * d:6 * d] * (y1_ref[...] * w1 + y2_ref[...] * w2)
    if final:
        y = y * lax.rsqrt(jnp.mean(y * y, axis=-1, keepdims=True) + EPS) * fw_ref[...]
    o_ref[...] = y


def _combine(ys, dest, x2, route, mod, final_w, rows_per_mod, final):
    n, d = x2.shape
    tb = _pick(rows_per_mod, (256, 128))
    per = rows_per_mod // tb
    return pl.pallas_call(
        functools.partial(_combine_kernel, d=d, tb=tb, final=final),
        grid=(n // tb,),
        in_specs=[pl.BlockSpec((None, 1, 2 * tb), lambda i: (i, 0, 0), memory_space=pltpu.SMEM),
                  pl.BlockSpec(memory_space=pl.ANY),
                  pl.BlockSpec((tb, d), lambda i: (i, 0)),
                  pl.BlockSpec((tb, LANES), lambda i: (i, 0)),
                  pl.BlockSpec((None, 1, 6 * d), lambda i: (i // per, 0, 0)),
                  pl.BlockSpec((1, d), lambda i: (0, 0))],
        out_specs=pl.BlockSpec((tb, d), lambda i: (i, 0)),
        out_shape=jax.ShapeDtypeStruct((n, d), F32),
        scratch_shapes=[pltpu.VMEM((tb, d), F32), pltpu.VMEM((tb, d), F32), pltpu.SemaphoreType.DMA(())],
        compiler_params=_cparams(("arbitrary",)),
        name="moe_combine_rows",
    )(dest.reshape(n // tb, 1, 2 * tb), ys, x2, route, mod, final_w)


def _moe_ffn(x, mod, nw, final_w, router_w, router_b, wg, wu, wd, final):
    b, t, d = x.shape
    n = b * t
    x2 = x.reshape(n, d)
    h, route, counts = _route(x2, mod, nw, router_w, router_b, t)
    mb = _pick(n, (1024, 512, 256, 128))
    cnt = counts[0, :N_EXPERTS].astype(jnp.int32)
    nblk_e = (cnt + mb - 1) // mb
    blk_end = jnp.cumsum(nblk_e)
    pad_start = (blk_end - nblk_e) * mb
    nblk = (2 * n) // mb + N_EXPERTS
    bi = jnp.arange(nblk, dtype=jnp.int32)
    total_blk = blk_end[-1]
    blk_v = (bi < total_blk).astype(jnp.int32)
    clamped = jnp.minimum(bi, total_blk - 1).astype(jnp.int32)
    blk_e = jnp.sum((clamped[:, None] >= blk_end[None, :]).astype(jnp.int32), axis=1)
    blk_e = jnp.minimum(blk_e, N_EXPERTS - 1).astype(jnp.int32)
    e12 = route[:, R_E1:R_E2 + 1].astype(jnp.int32)
    rank12 = route[:, R_RANK1:R_RANK2 + 1].astype(jnp.int32)
    dest = (pad_start[e12] + rank12).reshape(n * 2)
    xs = _dispatch(h, dest, nblk * mb)
    ys = _experts(xs, blk_e, clamped, blk_v, wg, wu, wd, mb)
    out = _combine(ys, dest, x2, route, mod, final_w, t, final)
    return out.reshape(b, t, d)


def _pad_in_weight(w):
    d = w.shape[0]
    ab_end = 4 * DN_DIM + 4 * DN_HEADS
    return jnp.concatenate([w[:, :ab_end], jnp.zeros((d, SEG_AB[1] - ab_end), w.dtype), w[:, ab_end:]],
                           axis=1).astype(BF16)


def kernel(x, c, ctx, c_ctx, w_mod, b_mod, norm1_w, norm2_w, w_in, dn_conv_w, dn_a_log, dn_dt_bias, dn_norm_w, at_qnorm_w, at_knorm_w, w_out, ffn_w_gate, ffn_w_up, ffn_w_down, moe_router_w, moe_router_b, moe_w_gate, moe_w_up, moe_w_down, final_norm_w):
    b, t, d = x.shape
    depth = w_mod.shape[0]
    xc = ctx
    rows = -(-(b + 1) // SUBLANES) * SUBLANES
    c_rows = jnp.zeros((rows, d), F32).at[:b].set(c).at[b].set(c_ctx)
    mod_all = _modulation(c_rows, w_mod, b_mod)
    final_w = final_norm_w.reshape(1, d)
    zero_state = jnp.zeros((b, 2, DN_GROUPS, MXU_DIM, MXU_DIM), F32)

    for layer in range(depth):
        last = layer == depth - 1
        mod = mod_all[layer, :b].reshape(b, 1, 6 * d)
        mod_c = jnp.broadcast_to(mod_all[layer, b].reshape(1, 1, 6 * d), (b, 1, 6 * d))
        n1 = norm1_w[layer].reshape(1, d)
        n2 = norm2_w[layer].reshape(1, d)
        w_pad = _pad_in_weight(w_in[layer])
        w_o = w_out[layer].astype(BF16)

        qkv, z, ab, atq, atk, atv = _in_projection(x, mod, n1, w_pad)
        qkv_c, z_c, ab_c, atq_c, atk_c, atv_c = _in_projection(xc, mod_c, n1, w_pad)

        prep_c = _dn_prepare(qkv_c, ab_c, dn_conv_w[layer], dn_a_log[layer], dn_dt_bias[layer])
        of_c, ob_c, s_ctx = _dn_scan(prep_c, zero_state)
        prep = _dn_prepare(qkv, ab, dn_conv_w[layer], dn_a_log[layer], dn_dt_bias[layer])
        o_f, o_b, _ = _dn_scan(prep, s_ctx)

        q_l, kd_l, vd_l = _attn_prepare(atq, atk, atv, at_qnorm_w[layer], at_knorm_w[layer], rope=True)
        q_c, kd_c, vd_c = _attn_prepare(atq_c, atk_c, atv_c, at_qnorm_w[layer], at_knorm_w[layer], rope=False)
        at = _flash_attention(q_l, jnp.concatenate([kd_l, kd_c], axis=2), jnp.concatenate([vd_l, vd_c], axis=3))

        x = _out_projection(o_f, o_b, z, at, x, mod, dn_norm_w[layer], w_o)
        if not last:
            at_c = _flash_attention(q_c, kd_c, vd_c)
            xc = _out_projection(of_c, ob_c, z_c, at_c, xc, mod_c, dn_norm_w[layer], w_o)

        if layer % 2 == 0:
            i = layer // 2
            wg, wu, wd = (w[i].astype(BF16) for w in (ffn_w_gate, ffn_w_up, ffn_w_down))
            x = _dense_ffn(x, mod, n2, final_w, wg, wu, wd, last)
            if not last:
                xc = _dense_ffn(xc, mod_c, n2, final_w, wg, wu, wd, False)
        else:
            j = layer // 2
            wg, wu, wd = (w[j].astype(BF16) for w in (moe_w_gate, moe_w_up, moe_w_down))
            x = _moe_ffn(x, mod, n2, final_w, moe_router_w[j], moe_router_b[j], wg, wu, wd, last)
            if not last:
                xc = _moe_ffn(xc, mod_c, n2, final_w, moe_router_w[j], moe_router_b[j], wg, wu, wd, False)
    return x
```

```python
import functools
import math

import jax
import jax.numpy as jnp
from jax import lax
from jax.experimental import pallas as pl
from jax.experimental.pallas import tpu as pltpu

F32 = jnp.float32
BF16 = jnp.bfloat16
EPS = 1e-6
LOG2E = math.log2(math.e)

DN_HEADS = 8
HEAD_DIM = 64
DN_DIM = DN_HEADS * HEAD_DIM
DN_CONV_W = 5
DN_CHUNK = 64
AT_Q_HEADS = 8
AT_KV_HEADS = 2
AT_GROUP = AT_Q_HEADS // AT_KV_HEADS
AT_Q_DIM = AT_Q_HEADS * HEAD_DIM
AT_KV_DIM = AT_KV_HEADS * HEAD_DIM
GRID_W = 64
ROPE_THETA = 10000.0
N_EXPERTS = 8

LANES = 128
SUBLANES = 8
MXU_DIM = 256
VMEM_LIMIT = 52 * 1024 * 1024

HEADS_PER_GROUP = MXU_DIM // HEAD_DIM
DN_GROUPS = DN_HEADS // HEADS_PER_GROUP
FLASH_ROWS = 256
FLASH_BATCH = 8
PV_ROWS = HEAD_DIM + 16
DMA_UNROLL = 8

SEG_QKV = (0, 3 * DN_DIM)
SEG_Z = (3 * DN_DIM, 4 * DN_DIM)
SEG_AB = (4 * DN_DIM, 4 * DN_DIM + LANES)
SEG_ATQ = (SEG_AB[1], SEG_AB[1] + AT_Q_DIM)
SEG_ATK = (SEG_ATQ[1], SEG_ATQ[1] + AT_KV_DIM)
SEG_ATV = (SEG_ATK[1], SEG_ATK[1] + AT_KV_DIM)
IN_PAD = SEG_ATV[1]
SEGS = (SEG_QKV, SEG_Z, SEG_AB, SEG_ATQ, SEG_ATK, SEG_ATV)


def _cparams(sem):
    return pltpu.CompilerParams(dimension_semantics=sem, vmem_limit_bytes=VMEM_LIMIT)


def _bdot(a, b):
    return jnp.dot(a.astype(BF16), b.astype(BF16), preferred_element_type=F32)


def _bdot_nt(a, b):
    return lax.dot_general(a.astype(BF16), b.astype(BF16), (((1,), (1,)), ((), ())),
                           preferred_element_type=F32)


def _bdot_tn(a, b):
    return lax.dot_general(a.astype(BF16), b.astype(BF16), (((0,), (0,)), ((), ())),
                           preferred_element_type=F32)


def _split3(x):
    hi = x.astype(BF16)
    r1 = x - hi.astype(F32)
    mid = r1.astype(BF16)
    lo = (r1 - mid.astype(F32)).astype(BF16)
    return hi, mid, lo


def _dot_sel(x, sel):
    dot = functools.partial(jnp.dot, preferred_element_type=F32)
    hi, mid, lo = _split3(x)
    return dot(hi, sel) + dot(mid, sel) + dot(lo, sel)


def _dot_sel_left(sel, x):
    dot = functools.partial(jnp.dot, preferred_element_type=F32)
    hi, mid, lo = _split3(x)
    return dot(sel, hi) + dot(sel, mid) + dot(sel, lo)


def _sigmoid(x):
    return 1.0 / (1.0 + jnp.exp(-x))


def _silu(x):
    return x * _sigmoid(x)


def _norm_modulate(x, nw, shift, scale):
    y = x * lax.rsqrt(jnp.mean(x * x, axis=-1, keepdims=True) + EPS)
    return (y * nw) * (1.0 + scale) + shift


def _pick(n, options):
    for o in options:
        if n % o == 0:
            return o
    raise ValueError(f"no tile in {options} divides {n}")


def _mod_kernel(c_ref, w_ref, b_ref, o_ref):
    o_ref[...] = _bdot(_silu(c_ref[...]), w_ref[...]) + b_ref[...]


def _modulation(c_rows, w_mod, b_mod):
    nl, d, n6 = w_mod.shape
    rows = c_rows.shape[0]
    tn = _pick(n6, (1536, 1024, 512, 256, 128))
    return pl.pallas_call(
        _mod_kernel,
        grid=(nl, n6 // tn),
        in_specs=[pl.BlockSpec((rows, d), lambda l, j: (0, 0)),
                  pl.BlockSpec((None, d, tn), lambda l, j: (l, 0, j)),
                  pl.BlockSpec((None, 1, tn), lambda l, j: (l, 0, j))],
        out_specs=pl.BlockSpec((None, rows, tn), lambda l, j: (l, 0, j)),
        out_shape=jax.ShapeDtypeStruct((nl, rows, n6), F32),
        compiler_params=_cparams(("parallel", "parallel")),
        name="adaln_modulation",
    )(c_rows, w_mod, b_mod.reshape(nl, 1, n6))


def _inproj_kernel(x_ref, mod_ref, nw_ref, w_ref, *out_refs, d):
    h = _norm_modulate(x_ref[...], nw_ref[...], mod_ref[:, 0:d], mod_ref[:, d:2 * d]).astype(BF16)
    for (a, b), o_ref in zip(SEGS, out_refs):
        o_ref[...] = jnp.dot(h, w_ref[:, a:b], preferred_element_type=F32)


def _in_projection(x, mod, nw, w_pad):
    b, t, d = x.shape
    tm = _pick(t, (512, 256, 128))
    return pl.pallas_call(
        functools.partial(_inproj_kernel, d=d),
        grid=(b, t // tm),
        in_specs=[pl.BlockSpec((None, tm, d), lambda i, j: (i, j, 0)),
                  pl.BlockSpec((None, 1, 6 * d), lambda i, j: (i, 0, 0)),
                  pl.BlockSpec((1, d), lambda i, j: (0, 0)),
                  pl.BlockSpec((d, IN_PAD), lambda i, j: (0, 0))],
        out_specs=[pl.BlockSpec((None, tm, s1 - s0), lambda i, j: (i, j, 0)) for s0, s1 in SEGS],
        out_shape=[jax.ShapeDtypeStruct((b, t, s1 - s0), F32) for s0, s1 in SEGS],
        compiler_params=_cparams(("parallel", "parallel")),
        name="norm_in_projection",
    )(x, mod, nw, w_pad)


def _dn_prep_kernel(xp_ref, x_ref, xn_ref, ab_ref, cw_ref, gp_ref, head_ones_ref, cum_f_ref, cum_b_ref,
                    expand_ref, q_ref, k_ref, v_ref, bf_ref, gf_ref, bb_ref, gb_ref, gc_ref, xe_ref, *, tm):
    i = pl.program_id(1)
    halo = SUBLANES
    pad = (DN_CONV_W - 1) // 2
    zeros = jnp.zeros((halo, 3 * DN_DIM), F32)
    xe_ref[0:halo, :] = jnp.where(i > 0, xp_ref[...], zeros)
    xe_ref[halo:halo + tm, :] = x_ref[...]
    xe_ref[halo + tm:halo + tm + halo, :] = jnp.where(i < pl.num_programs(1) - 1, xn_ref[...], zeros)
    acc = cw_ref[0:1, :] * xe_ref[halo - pad:halo - pad + tm, :]
    for j in range(1, DN_CONV_W):
        acc = acc + cw_ref[j:j + 1, :] * xe_ref[halo - pad + j:halo - pad + j + tm, :]
    qkv = _silu(acc)
    ones = head_ones_ref[...]
    q = qkv[:, 0:DN_DIM]
    k = qkv[:, DN_DIM:2 * DN_DIM]
    q_ref[...] = q * lax.rsqrt(_dot_sel(q * q, ones) + EPS) * (HEAD_DIM ** -0.5)
    k_ref[...] = k * lax.rsqrt(_dot_sel(k * k, ones) + EPS)
    v_ref[...] = qkv[:, 2 * DN_DIM:3 * DN_DIM]

    ab = ab_ref[...]
    lane = lax.broadcasted_iota(jnp.int32, ab.shape, 1)
    z = ab + gp_ref[1:2, :]
    softplus = jnp.maximum(z, 0.0) + jnp.log(1.0 + jnp.exp(-jnp.abs(z)))
    g = -jnp.exp(gp_ref[0:1, :]) * softplus
    g = jnp.where(lane < 2 * DN_HEADS, g, 0.0)
    gc = jnp.where(lane < DN_HEADS, _dot_sel_left(cum_f_ref[...], g), _dot_sel_left(cum_b_ref[...], g))
    gc_ref[...] = gc
    pieces = _split3(jnp.where(lane < 2 * DN_HEADS, gc, _sigmoid(ab)))
    for n, o_ref in enumerate((gf_ref, gb_ref, bf_ref, bb_ref)):
        e = expand_ref[n]
        o_ref[...] = sum(jnp.dot(p, e, preferred_element_type=F32) for p in pieces)


def _head_ones(width):
    idx = jnp.arange(width) // HEAD_DIM
    return (idx[:, None] == idx[None, :]).astype(BF16)


def _dn_prepare(qkv, ab, conv_w, a_log, dt_bias):
    b, t, c = qkv.shape
    tm = _pick(t, (256, 128, 64))
    nt = t // tm
    hb = tm // SUBLANES
    last8 = t // SUBLANES - 1
    gp = jnp.zeros((SUBLANES, LANES), F32)
    gp = gp.at[0, :2 * DN_HEADS].set(a_log.reshape(-1)).at[1, :2 * DN_HEADS].set(dt_bias.reshape(-1))
    r = jnp.arange(tm)
    same = (r[:, None] // DN_CHUNK) == (r[None, :] // DN_CHUNK)
    cum_f = (same & (r[None, :] <= r[:, None])).astype(BF16)
    cum_b = (same & (r[None, :] >= r[:, None])).astype(BF16)
    col_head = jnp.arange(DN_DIM) // HEAD_DIM
    src = jnp.arange(LANES)[:, None]
    expand = jnp.stack([(src == col_head[None, :] + off) for off in
                        (0, DN_HEADS, 2 * DN_HEADS, 3 * DN_HEADS)]).astype(BF16)
    wide = pl.BlockSpec((None, tm, DN_DIM), lambda i, j: (i, j, 0))
    const = lambda shape: pl.BlockSpec(shape, lambda i, j: (0,) * len(shape))
    outs = pl.pallas_call(
        functools.partial(_dn_prep_kernel, tm=tm),
        grid=(b, nt),
        in_specs=[pl.BlockSpec((None, SUBLANES, c), lambda i, j: (i, jnp.maximum(j * hb - 1, 0), 0)),
                  pl.BlockSpec((None, tm, c), lambda i, j: (i, j, 0)),
                  pl.BlockSpec((None, SUBLANES, c), lambda i, j: (i, jnp.minimum((j + 1) * hb, last8), 0)),
                  pl.BlockSpec((None, tm, LANES), lambda i, j: (i, j, 0)),
                  const((DN_CONV_W, c)), const((SUBLANES, LANES)), const((DN_DIM, DN_DIM)),
                  const((tm, tm)), const((tm, tm)), const((4, LANES, DN_DIM))],
        out_specs=[wide] * 7 + [pl.BlockSpec((None, tm, LANES), lambda i, j: (i, j, 0))],
        out_shape=[jax.ShapeDtypeStruct((b, t, DN_DIM), F32)] * 7 + [jax.ShapeDtypeStruct((b, t, LANES), F32)],
        scratch_shapes=[pltpu.VMEM((tm + 2 * SUBLANES, c), F32)],
        compiler_params=_cparams(("parallel", "parallel")),
        name="deltanet_prepare",
    )(qkv, qkv, qkv, ab, conv_w, gp, _head_ones(DN_DIM), cum_f, cum_b, expand)
    return outs


def _tile4(x):
    return jnp.concatenate([x] * HEADS_PER_GROUP, axis=0)


def _group_masks():
    n = HEADS_PER_GROUP * DN_CHUNK
    row = lax.broadcasted_iota(jnp.int32, (n, n), 0)
    col = lax.broadcasted_iota(jnp.int32, (n, n), 1)
    return row, col, (row // DN_CHUNK) == (col // DN_CHUNK)


def _collapse(full, same_head):
    kept = jnp.where(same_head, full, 0.0)
    out = kept[0:DN_CHUNK]
    for h in range(1, HEADS_PER_GROUP):
        out = out + kept[h * DN_CHUNK:(h + 1) * DN_CHUNK]
    return out


def _dn_local_kernel(q_ref, k_ref, v_ref, btf_ref, gxf_ref, btb_ref, gxb_ref, gr_ref,
                     uf_ref, wf_ref, af_ref, ub_ref, wb_ref, ab_ref, *, chunks):
    c = DN_CHUNK
    n = HEADS_PER_GROUP * c
    row, col, same_head = _group_masks()
    eye = (row == col).astype(F32)
    ci = lax.broadcasted_iota(jnp.int32, (c, n), 0)
    cj = lax.broadcasted_iota(jnp.int32, (c, n), 1) % c
    tri = {False: (ci >= cj, ci > cj), True: (ci <= cj, ci < cj)}

    def spread(x):
        return jnp.where(same_head, _tile4(x), 0.0)

    def joins(s):
        return ((row // (2 * s)) == (col // (2 * s))) & ((row // s) != (col // s))

    chains = []
    for cc in range(chunks):
        rs = slice(cc * c, (cc + 1) * c)
        for d, (bt_ref, gx_ref, outs) in enumerate(((btf_ref, gxf_ref, (uf_ref, wf_ref, af_ref)),
                                                    (btb_ref, gxb_ref, (ub_ref, wb_ref, ab_ref)))):
            for g in range(DN_GROUPS):
                chains.append((rs, slice(g * MXU_DIM, (g + 1) * MXU_DIM), d, g, cc, bt_ref, gx_ref, outs))

    a_mats, rhs_list = [], []
    for rs, sl, d, g, cc, bt_ref, gx_ref, outs in chains:
        incl, strict = tri[d == 1]
        q, k, v = q_ref[rs, sl], k_ref[rs, sl], v_ref[rs, sl]
        beta, gcx = bt_ref[rs, sl], gx_ref[rs, sl]
        diff = gcx - gr_ref[d, cc, :, sl]
        decay = jnp.where(incl, jnp.exp(jnp.where(incl, diff, 0.0)), 0.0)
        k_beta = k * beta
        prod = _bdot_nt(jnp.concatenate([k_beta, q], axis=0), spread(k)) * jnp.concatenate([decay, decay], axis=0)
        a_mats.append(spread(jnp.where(strict, prod[0:c], 0.0)))
        outs[2][rs, sl] = prod[c:2 * c]
        rhs_list.append(jnp.concatenate([spread(v * beta), spread(k_beta * jnp.exp(gcx))], axis=1))

    def half_rows(x, s, second):
        o = s if second else 0
        return jnp.concatenate([x[r0 + o:r0 + o + s] for r0 in range(0, n, 2 * s)], axis=0)

    def merge_rows(x, new_half, s, second):
        parts = []
        for bi, r0 in enumerate(range(0, n, 2 * s)):
            new = new_half[bi * s:(bi + 1) * s]
            parts += [x[r0:r0 + s], new] if second else [new, x[r0 + s:r0 + 2 * s]]
        return jnp.concatenate(parts, axis=0)

    second = [d == 0 for _, _, d, _, _, _, _, _ in chains]
    invs = [eye - jnp.where(joins(1), a, 0.0) for a in a_mats]
    s = 2
    while s < c:
        js = joins(s)
        if s < SUBLANES:
            mids = [_bdot(jnp.where(js, a, 0.0), inv) for a, inv in zip(a_mats, invs)]
            invs = [inv - _bdot(inv, mid) for inv, mid in zip(invs, mids)]
        else:
            zeros = jnp.zeros((n, n), F32)
            mids = [merge_rows(zeros, _bdot(half_rows(jnp.where(js, a, 0.0), s, sec), inv), s, sec)
                    for a, inv, sec in zip(a_mats, invs, second)]
            invs = [merge_rows(inv, half_rows(inv, s, sec) - _bdot(half_rows(inv, s, sec), mid), s, sec)
                    for inv, mid, sec in zip(invs, mids, second)]
        s *= 2

    for (rs, sl, d, g, cc, bt_ref, gx_ref, outs), inv, rhs in zip(chains, invs, rhs_list):
        sol = _bdot(_collapse(inv, same_head), rhs)
        outs[0][rs, sl] = sol[:, 0:n]
        outs[1][rs, sl] = sol[:, n:2 * n]


def _dn_local(prep):
    q, k, v, beta_f, gcx_f, beta_b, gcx_b, gc = prep
    b, t, _ = q.shape
    n = t // DN_CHUNK
    chunks = 4 if n % 4 == 0 else (2 if n % 2 == 0 else 1)
    rows = chunks * DN_CHUNK
    g_rows = gc[:, :, :2 * DN_HEADS].reshape(b, n, DN_CHUNK, 2, DN_HEADS)
    g_rows = g_rows.transpose(0, 3, 1, 4, 2).reshape(b, 2, n, 1, DN_DIM)
    wide = pl.BlockSpec((None, rows, DN_DIM), lambda i, j: (i, j, 0))
    return pl.pallas_call(
        functools.partial(_dn_local_kernel, chunks=chunks),
        grid=(b, n // chunks),
        in_specs=[wide] * 7 + [pl.BlockSpec((None, 2, chunks, 1, DN_DIM), lambda i, j: (i, 0, j, 0, 0))],
        out_specs=[wide] * 6,
        out_shape=[jax.ShapeDtypeStruct((b, t, DN_DIM), F32)] * 6,
        compiler_params=_cparams(("parallel", "parallel")),
        name="deltanet_chunk_local",
    )(q, k, v, beta_f, gcx_f, beta_b, gcx_b, g_rows)


def _dn_scan_kernel(qf, kf, gxf, uf, wf, af, qb, kb, gxb, ub, wb, ab, s0_ref, of_ref, ob_ref, sfin_ref, s_ref,
                    *, chunks):
    step = pl.program_id(1)
    c = DN_CHUNK

    @pl.when(step == 0)
    def _():
        s_ref[...] = s0_ref[...]

    _, _, same_head = _group_masks()
    dirs = ((qf, kf, gxf, uf, wf, af, of_ref), (qb, kb, gxb, ub, wb, ab, ob_ref))
    states = {(d, g): s_ref[d, g] for d in range(2) for g in range(DN_GROUPS)}
    for cc in range(chunks):
        for d, (q, k, gx, u, w, at, o_ref) in enumerate(dirs):
            reverse = d == 1
            ck = chunks - 1 - cc if reverse else cc
            rs = slice(ck * c, (ck + 1) * c)
            for g in range(DN_GROUPS):
                sl = slice(g * MXU_DIM, (g + 1) * MXU_DIM)
                state = states[(d, g)]
                gcx = gx[rs, sl]
                g_last = gcx[0:1, :] if reverse else gcx[c - 1:c, :]
                q_dec = q[rs, sl] * jnp.exp(gcx)
                k_dec = k[rs, sl] * jnp.exp(g_last - gcx)
                ws = _bdot(jnp.concatenate([w[rs, sl], q_dec], axis=0), state)
                v_new = u[rs, sl] - ws[0:c]
                o_ref[rs, sl] = ws[c:2 * c] + _bdot(at[rs, sl], jnp.where(same_head, _tile4(v_new), 0.0))
                states[(d, g)] = state * jnp.exp(g_last) + jnp.where(same_head, _bdot_tn(k_dec, v_new), 0.0)
    for (d, g), state in states.items():
        s_ref[d, g] = state

    @pl.when(step == pl.num_programs(1) - 1)
    def _():
        sfin_ref[...] = s_ref[...]


def _dn_scan(prep, s0):
    q, k, _, _, gcx_f, _, gcx_b, _ = prep
    u_f, w_f, a_f, u_b, w_b, a_b = _dn_local(prep)
    b, t, _ = q.shape
    n = t // DN_CHUNK
    chunks = 4 if n % 4 == 0 else (2 if n % 2 == 0 else 1)
    rows = chunks * DN_CHUNK
    steps = n // chunks
    fwd = pl.BlockSpec((None, rows, DN_DIM), lambda i, j: (i, j, 0))
    bwd = pl.BlockSpec((None, rows, DN_DIM), lambda i, j: (i, steps - 1 - j, 0))
    state = pl.BlockSpec((None, 2, DN_GROUPS, MXU_DIM, MXU_DIM), lambda i, j: (i, 0, 0, 0, 0))
    o_f, o_b, s_fin = pl.pallas_call(
        functools.partial(_dn_scan_kernel, chunks=chunks),
        grid=(b, steps),
        in_specs=[fwd] * 6 + [bwd] * 6 + [state],
        out_specs=[fwd, bwd, state],
        out_shape=[jax.ShapeDtypeStruct((b, t, DN_DIM), F32)] * 2
        + [jax.ShapeDtypeStruct((b, 2, DN_GROUPS, MXU_DIM, MXU_DIM), F32)],
        scratch_shapes=[pltpu.VMEM((2, DN_GROUPS, MXU_DIM, MXU_DIM), F32)],
        compiler_params=_cparams(("parallel", "arbitrary")),
        name="deltanet_scan",
    )(q, k, gcx_f, u_f, w_f, a_f, q, k, gcx_b, u_b, w_b, a_b, s0)
    return o_f, o_b, s_fin


def _rope(x, cos_t, sin_t):
    width = x.shape[-1]
    half = HEAD_DIM // 2
    lane = lax.broadcasted_iota(jnp.int32, x.shape, 1)
    partner = jnp.where(lane % HEAD_DIM < half, pltpu.roll(x, width - half, 1), pltpu.roll(x, half, 1))
    reps = width // LANES
    cos_w = jnp.concatenate([cos_t] * reps, axis=1) if reps > 1 else cos_t
    sin_w = jnp.concatenate([sin_t] * reps, axis=1) if reps > 1 else sin_t
    return x * cos_w + partner * sin_w


def _head_rmsnorm(x, ones, w):
    ms = _dot_sel(x * x, ones) * (1.0 / HEAD_DIM)
    return x * lax.rsqrt(ms + EPS) * w


def _attn_prep_kernel(q_ref, k_ref, v_ref, qw_ref, kw_ref, cos_ref, sin_ref, ones_q_ref, ones_k_ref,
                      qo_ref, ko_ref, vo_ref, *, rope):
    q = _head_rmsnorm(q_ref[...], ones_q_ref[...], qw_ref[...])
    k = _head_rmsnorm(k_ref[...], ones_k_ref[...], kw_ref[...])
    if rope:
        q = _rope(q, cos_ref[...], sin_ref[...])
        k = _rope(k, cos_ref[...], sin_ref[...])
    qo_ref[...] = (q * (HEAD_DIM ** -0.5 * LOG2E)).astype(BF16)
    v = v_ref[...]
    lane = lax.broadcasted_iota(jnp.int32, k.shape, 1)
    first = lane < HEAD_DIM
    k_sw = pltpu.roll(k, HEAD_DIM, 1)
    v_sw = pltpu.roll(v, HEAD_DIM, 1)
    ko_ref[0] = jnp.where(first, k, k_sw).astype(BF16)
    ko_ref[1] = jnp.where(first, k_sw, k).astype(BF16)
    vo_ref[0] = jnp.where(first, v, 1.0).T[0:PV_ROWS, :].astype(BF16)
    vo_ref[1] = jnp.where(first, v_sw, 1.0).T[0:PV_ROWS, :].astype(BF16)


def _rope_tables(t):
    rows = t // GRID_W
    row_pos = jnp.repeat(jnp.arange(rows, dtype=F32), GRID_W, total_repeat_length=t)
    col_pos = jnp.tile(jnp.arange(GRID_W, dtype=F32), rows)
    n_freq = HEAD_DIM // 4
    freqs = ROPE_THETA ** (-jnp.arange(n_freq, dtype=F32) / n_freq)
    ang = jnp.concatenate([row_pos[:, None] * freqs, col_pos[:, None] * freqs], axis=-1)
    cos, sin = jnp.cos(ang), jnp.sin(ang)
    reps = LANES // HEAD_DIM
    return (jnp.tile(jnp.concatenate([cos, cos], axis=-1), (1, reps)),
            jnp.tile(jnp.concatenate([-sin, sin], axis=-1), (1, reps)))


def _attn_prepare(atq, atk, atv, qn_w, kn_w, rope):
    b, t, _ = atq.shape
    tm = _pick(t, (512, 256, 128))
    if rope:
        cos_t, sin_t = _rope_tables(t)
    else:
        cos_t = sin_t = jnp.zeros((t, LANES), F32)
    qw = jnp.tile(qn_w, AT_Q_HEADS).reshape(1, AT_Q_DIM)
    kw = jnp.tile(kn_w, AT_KV_HEADS).reshape(1, AT_KV_DIM)
    const = lambda shape: pl.BlockSpec(shape, lambda i, j: (0,) * len(shape))
    tab = pl.BlockSpec((tm, LANES), lambda i, j: (j, 0))
    kv_out = pl.BlockSpec((None, AT_KV_HEADS, tm, AT_KV_DIM), lambda i, j: (i, 0, j, 0))
    return pl.pallas_call(
        functools.partial(_attn_prep_kernel, rope=rope),
        grid=(b, t // tm),
        in_specs=[pl.BlockSpec((None, tm, AT_Q_DIM), lambda i, j: (i, j, 0)),
                  pl.BlockSpec((None, tm, AT_KV_DIM), lambda i, j: (i, j, 0)),
                  pl.BlockSpec((None, tm, AT_KV_DIM), lambda i, j: (i, j, 0)),
                  const((1, AT_Q_DIM)), const((1, AT_KV_DIM)), tab, tab,
                  const((AT_Q_DIM, AT_Q_DIM)), const((AT_KV_DIM, AT_KV_DIM))],
        out_specs=[pl.BlockSpec((None, tm, AT_Q_DIM), lambda i, j: (i, j, 0)), kv_out,
                   pl.BlockSpec((None, AT_KV_HEADS, PV_ROWS, tm), lambda i, j: (i, 0, 0, j))],
        out_shape=[jax.ShapeDtypeStruct((b, t, AT_Q_DIM), BF16),
                   jax.ShapeDtypeStruct((b, AT_KV_HEADS, t, AT_KV_DIM), BF16),
                   jax.ShapeDtypeStruct((b, AT_KV_HEADS, PV_ROWS, t), BF16)],
        compiler_params=_cparams(("parallel", "parallel")),
        name="attention_prepare",
    )(atq, atk, atv, qw, kw, cos_t, sin_t, _head_ones(AT_Q_DIM), _head_ones(AT_KV_DIM))


def _flash_kernel(q_ref, k_ref, vt_ref, o_ref, qs_ref, m_ref, acc_ref, *, tq):
    j = pl.program_id(3)

    @pl.when(j == 0)
    def _():
        q = q_ref[...]
        lane = lax.broadcasted_iota(jnp.int32, q.shape, 1)
        for h in range(AT_GROUP):
            qs_ref[h * tq:(h + 1) * tq, :] = jnp.where(lane // HEAD_DIM == h, q, jnp.zeros_like(q))
        m_ref[...] = jnp.full(m_ref.shape, -jnp.inf, F32)
        acc_ref[...] = jnp.zeros(acc_ref.shape, F32)

    k = k_ref[...]
    k2 = jnp.concatenate([k, k], axis=1)
    vt = vt_ref[...]
    all_blocks = [slice(r0, r0 + FLASH_ROWS) for r0 in range(0, AT_GROUP * tq, FLASH_ROWS)]
    for b0 in range(0, len(all_blocks), FLASH_BATCH):
        blocks = all_blocks[b0:b0 + FLASH_BATCH]
        scores = [lax.dot_general(k2, qs_ref[cols, :], (((1,), (1,)), ((), ())), preferred_element_type=F32)
                  for cols in blocks]
        m_prev = [m_ref[:, cols] for cols in blocks]
        acc_prev = [acc_ref[:, cols] for cols in blocks]
        m_next = [jnp.maximum(mp, jnp.max(s, axis=0, keepdims=True)) for mp, s in zip(m_prev, scores)]
        probs = [jnp.exp2(s - mn[0:1, :]).astype(BF16) for s, mn in zip(scores, m_next)]
        pv = [jnp.dot(vt, p, preferred_element_type=F32) for p in probs]
        for cols, mp, mn, ap, x in zip(blocks, m_prev, m_next, acc_prev, pv):
            acc_ref[:, cols] = jnp.exp2(mp[0:1, :] - mn[0:1, :]) * ap + x
            m_ref[:, cols] = mn

    @pl.when(j == pl.num_programs(3) - 1)
    def _():
        def normalised(h):
            a = acc_ref[:, h * tq:(h + 1) * tq]
            return a[0:HEAD_DIM, :] / a[HEAD_DIM:HEAD_DIM + 1, :]

        pairs = [jnp.concatenate([normalised(2 * p2), normalised(2 * p2 + 1)], axis=0).T
                 for p2 in range(AT_GROUP // 2)]
        o_ref[...] = jnp.concatenate(pairs, axis=1)


def _flash_attention(q, kd, vd):
    b, t, _ = q.shape
    s = kd.shape[2]
    tq = _pick(t, (1024, 512, 256, 128))
    tk = _pick(s, (1408, 768, 512, 256, 128))
    gw = AT_GROUP * HEAD_DIM
    return pl.pallas_call(
        functools.partial(_flash_kernel, tq=tq),
        grid=(b, AT_KV_HEADS, t // tq, s // tk),
        in_specs=[pl.BlockSpec((None, tq, gw), lambda i, g, a, j: (i, a, g)),
                  pl.BlockSpec((None, None, tk, AT_KV_DIM), lambda i, g, a, j: (i, g, j, 0)),
                  pl.BlockSpec((None, None, PV_ROWS, tk), lambda i, g, a, j: (i, g, 0, j))],
        out_specs=pl.BlockSpec((None, tq, gw), lambda i, g, a, j: (i, a, g)),
        out_shape=jax.ShapeDtypeStruct((b, t, AT_Q_DIM), F32),
        scratch_shapes=[pltpu.VMEM((AT_GROUP * tq, gw), BF16),
                        pltpu.VMEM((SUBLANES, AT_GROUP * tq), F32),
                        pltpu.VMEM((PV_ROWS, AT_GROUP * tq), F32)],
        compiler_params=_cparams(("parallel", "parallel", "parallel", "arbitrary")),
        name="gqa_flash_attention",
    )(q, kd, vd)


def _outproj_kernel(of_ref, ob_ref, z_ref, at_ref, x_ref, mod_ref, nw_ref, ones_ref, w_ref, o_ref, *, d):
    o = of_ref[...] + ob_ref[...]
    ms = _dot_sel(o * o, ones_ref[...]) * (1.0 / HEAD_DIM)
    dn = o * lax.rsqrt(ms + EPS) * nw_ref[...] * _silu(z_ref[...])
    mix = _bdot(dn, w_ref[0:DN_DIM, :]) + _bdot(at_ref[...], w_ref[DN_DIM:DN_DIM + AT_Q_DIM, :])
    o_ref[...] = x_ref[...] + mod_ref[:, 2 * d:3 * d] * mix


def _out_projection(o_f, o_b, z, at, x, mod, dn_norm_w, w_out):
    b, t, d = x.shape
    tm = _pick(t, (512, 256, 128))
    half = pl.BlockSpec((None, tm, DN_DIM), lambda i, j: (i, j, 0))
    full = pl.BlockSpec((None, tm, d), lambda i, j: (i, j, 0))
    const = lambda shape: pl.BlockSpec(shape, lambda i, j: (0,) * len(shape))
    return pl.pallas_call(
        functools.partial(_outproj_kernel, d=d),
        grid=(b, t // tm),
        in_specs=[half, half, half, half, full, pl.BlockSpec((None, 1, 6 * d), lambda i, j: (i, 0, 0)),
                  const((1, DN_DIM)), const((DN_DIM, DN_DIM)), const((DN_DIM + AT_Q_DIM, d))],
        out_specs=full,
        out_shape=jax.ShapeDtypeStruct((b, t, d), F32),
        compiler_params=_cparams(("parallel", "parallel")),
        name="mix_out_projection",
    )(o_f, o_b, z, at, x, mod, jnp.tile(dn_norm_w, DN_HEADS).reshape(1, DN_DIM), _head_ones(DN_DIM), w_out)


def _ffn_kernel(x_ref, mod_ref, nw_ref, fw_ref, wg_ref, wu_ref, wd_ref, o_ref, h_ref, acc_ref, *, d, final):
    f = pl.program_id(2)

    @pl.when(f == 0)
    def _():
        h_ref[...] = _norm_modulate(x_ref[...], nw_ref[...], mod_ref[:, 3 * d:4 * d],
                                    mod_ref[:, 4 * d:5 * d]).astype(BF16)
        acc_ref[...] = jnp.zeros(acc_ref.shape, F32)

    h = h_ref[...]
    gate = jnp.dot(h, wg_ref[...], preferred_element_type=F32)
    up = jnp.dot(h, wu_ref[...], preferred_element_type=F32)
    acc_ref[...] += jnp.dot((_silu(gate) * up).astype(BF16), wd_ref[...], preferred_element_type=F32)

    @pl.when(f == pl.num_programs(2) - 1)
    def _():
        y = x_ref[...] + mod_ref[:, 5 * d:6 * d] * acc_ref[...]
        if final:
            y = y * lax.rsqrt(jnp.mean(y * y, axis=-1, keepdims=True) + EPS) * fw_ref[...]
        o_ref[...] = y


def _dense_ffn(x, mod, nw, final_w, wg, wu, wd, final):
    b, t, d = x.shape
    fdim = wg.shape[1]
    tm = _pick(t, (1024, 512, 256, 128))
    tf = _pick(fdim, (512, 256, 128))
    full = pl.BlockSpec((None, tm, d), lambda i, j, f: (i, j, 0))
    const = lambda shape: pl.BlockSpec(shape, lambda i, j, f: (0,) * len(shape))
    return pl.pallas_call(
        functools.partial(_ffn_kernel, d=d, final=final),
        grid=(b, t // tm, fdim // tf),
        in_specs=[full, pl.BlockSpec((None, 1, 6 * d), lambda i, j, f: (i, 0, 0)), const((1, d)), const((1, d)),
                  pl.BlockSpec((d, tf), lambda i, j, f: (0, f)),
                  pl.BlockSpec((d, tf), lambda i, j, f: (0, f)),
                  pl.BlockSpec((tf, d), lambda i, j, f: (f, 0))],
        out_specs=full,
        out_shape=jax.ShapeDtypeStruct((b, t, d), F32),
        scratch_shapes=[pltpu.VMEM((tm, d), BF16), pltpu.VMEM((tm, d), F32)],
        compiler_params=_cparams(("parallel", "parallel", "arbitrary")),
        name="dense_swiglu",
    )(x, mod, nw, final_w, wg, wu, wd)


R_E1, R_E2, R_W1, R_W2, R_RANK1, R_RANK2 = range(6)


def _router_kernel(x_ref, mod_ref, nw_ref, rw_ref, rb_ref, tri_ref, h_ref, route_ref, cnt_ref, carry_ref, *, d):
    @pl.when(pl.program_id(0) == 0)
    def _():
        carry_ref[...] = jnp.zeros(carry_ref.shape, F32)

    h = _norm_modulate(x_ref[...], nw_ref[...], mod_ref[:, 3 * d:4 * d], mod_ref[:, 4 * d:5 * d])
    h_ref[...] = h
    logits = _bdot(h, rw_ref[...]) + rb_ref[...]
    lane = lax.broadcasted_iota(jnp.int32, logits.shape, 1).astype(F32)
    neg = jnp.float32(-jnp.inf)
    lg = jnp.where(lane < N_EXPERTS, logits, neg)
    m1 = jnp.max(lg, axis=1, keepdims=True)
    i1 = jnp.min(jnp.where(lg == m1, lane, float(LANES)), axis=1, keepdims=True)
    hot1 = lane == i1
    lg2 = jnp.where(hot1, neg, lg)
    m2 = jnp.max(lg2, axis=1, keepdims=True)
    i2 = jnp.min(jnp.where(lg2 == m2, lane, float(LANES)), axis=1, keepdims=True)
    hot2 = lane == i2
    e2 = jnp.exp(m2 - m1)
    w1 = 1.0 / (1.0 + e2)
    w2 = e2 / (1.0 + e2)
    member = jnp.where(hot1 | hot2, 1.0, 0.0)
    before = jnp.dot(tri_ref[...], member.astype(BF16), preferred_element_type=F32) + carry_ref[0:1, :]
    rank1 = jnp.sum(jnp.where(hot1, before, 0.0), axis=1, keepdims=True)
    rank2 = jnp.sum(jnp.where(hot2, before, 0.0), axis=1, keepdims=True)
    rec = jnp.zeros(logits.shape, F32)
    for ln, val in ((R_E1, i1), (R_E2, i2), (R_W1, w1), (R_W2, w2), (R_RANK1, rank1), (R_RANK2, rank2)):
        rec = jnp.where(lane == float(ln), val, rec)
    route_ref[...] = rec
    tb = member.shape[0]
    total = before[tb - 1:tb, :] + member[tb - 1:tb, :]
    carry_ref[...] = jnp.broadcast_to(total, carry_ref.shape)
    cnt_ref[...] = jnp.broadcast_to(total, cnt_ref.shape)


def _route(x2, mod, nw, router_w, router_b, rows_per_mod):
    n, d = x2.shape
    tb = _pick(rows_per_mod, (512, 256, 128))
    per = rows_per_mod // tb
    rw = jnp.zeros((d, LANES), BF16).at[:, :N_EXPERTS].set(router_w.astype(BF16))
    rb = jnp.zeros((1, LANES), F32).at[0, :N_EXPERTS].set(router_b)
    r = jnp.arange(tb)
    tri = (r[None, :] < r[:, None]).astype(BF16)
    const = lambda shape: pl.BlockSpec(shape, lambda i: (0,) * len(shape))
    return pl.pallas_call(
        functools.partial(_router_kernel, d=d),
        grid=(n // tb,),
        in_specs=[pl.BlockSpec((tb, d), lambda i: (i, 0)),
                  pl.BlockSpec((None, 1, 6 * d), lambda i: (i // per, 0, 0)),
                  const((1, d)), const((d, LANES)), const((1, LANES)), const((tb, tb))],
        out_specs=[pl.BlockSpec((tb, d), lambda i: (i, 0)), pl.BlockSpec((tb, LANES), lambda i: (i, 0)),
                   const((SUBLANES, LANES))],
        out_shape=[jax.ShapeDtypeStruct((n, d), F32), jax.ShapeDtypeStruct((n, LANES), F32),
                   jax.ShapeDtypeStruct((SUBLANES, LANES), F32)],
        scratch_shapes=[pltpu.VMEM((SUBLANES, LANES), F32)],
        compiler_params=_cparams(("arbitrary",)),
        name="moe_router_top2",
    )(x2, mod, nw, rw, rb, tri)


def _dispatch_kernel(dest_ref, h_ref, xs_in_ref, xs_ref, sem, *, tb):
    del xs_in_ref

    def copy(t, kk):
        return pltpu.make_async_copy(h_ref.at[pl.ds(t, 1), :],
                                     xs_ref.at[pl.ds(dest_ref[0, 2 * t + kk], 1), :], sem)

    def start(t, c):
        copy(t, 0).start(priority=0)
        copy(t, 1).start(priority=1)
        return c

    def wait(t, c):
        copy(t, 0).wait()
        copy(t, 1).wait()
        return c

    lax.fori_loop(0, tb, start, 0, unroll=DMA_UNROLL)
    lax.fori_loop(0, tb, wait, 0, unroll=DMA_UNROLL)


def _dispatch(h, dest, p_rows):
    n, d = h.shape
    tb = _pick(n, (256, 128))
    return pl.pallas_call(
        functools.partial(_dispatch_kernel, tb=tb),
        grid=(n // tb,),
        in_specs=[pl.BlockSpec((None, 1, 2 * tb), lambda i: (i, 0, 0), memory_space=pltpu.SMEM),
                  pl.BlockSpec((tb, d), lambda i: (i, 0)), pl.BlockSpec(memory_space=pl.ANY)],
        out_specs=pl.BlockSpec(memory_space=pl.ANY),
        out_shape=jax.ShapeDtypeStruct((p_rows, d), F32),
        scratch_shapes=[pltpu.SemaphoreType.DMA(())],
        input_output_aliases={2: 0},
        compiler_params=_cparams(("arbitrary",)),
        name="moe_dispatch_rows",
    )(dest.reshape(n // tb, 1, 2 * tb), h, jnp.zeros((p_rows, d), F32))


def _expert_kernel(be_ref, bx_ref, bv_ref, x_ref, wg_ref, wu_ref, wd_ref, o_ref, h_ref, acc_ref):
    i, f = pl.program_id(0), pl.program_id(1)
    valid = bv_ref[i] == 1
    last = f == pl.num_programs(1) - 1

    @pl.when(valid & (f == 0))
    def _():
        h_ref[...] = x_ref[...].astype(BF16)
        acc_ref[...] = jnp.zeros(acc_ref.shape, F32)

    @pl.when(valid)
    def _():
        h = h_ref[...]
        gate = jnp.dot(h, wg_ref[...], preferred_element_type=F32)
        up = jnp.dot(h, wu_ref[...], preferred_element_type=F32)
        acc_ref[...] += jnp.dot((_silu(gate) * up).astype(BF16), wd_ref[...], preferred_element_type=F32)

    @pl.when(valid & last)
    def _():
        o_ref[...] = acc_ref[...]

    @pl.when(jnp.logical_not(valid) & last)
    def _():
        o_ref[...] = jnp.zeros(o_ref.shape, F32)


def _experts(xs, blk_e, blk_x, blk_v, wg, wu, wd, mb):
    p_rows, d = xs.shape
    fdim = wg.shape[2]
    tf = _pick(fdim, (512, 256, 128))
    nf = fdim // tf
    fsel = lambda i, f, bv: jnp.where(bv[i] == 1, f, nf - 1)
    grid_spec = pltpu.PrefetchScalarGridSpec(
        num_scalar_prefetch=3,
        grid=(p_rows // mb, nf),
        in_specs=[pl.BlockSpec((mb, d), lambda i, f, be, bx, bv: (bx[i], 0)),
                  pl.BlockSpec((None, d, tf), lambda i, f, be, bx, bv: (be[i], 0, fsel(i, f, bv))),
                  pl.BlockSpec((None, d, tf), lambda i, f, be, bx, bv: (be[i], 0, fsel(i, f, bv))),
                  pl.BlockSpec((None, tf, d), lambda i, f, be, bx, bv: (be[i], fsel(i, f, bv), 0))],
        out_specs=pl.BlockSpec((mb, d), lambda i, f, be, bx, bv: (i, 0)),
        scratch_shapes=[pltpu.VMEM((mb, d), BF16), pltpu.VMEM((mb, d), F32)])
    return pl.pallas_call(
        _expert_kernel,
        grid_spec=grid_spec,
        out_shape=jax.ShapeDtypeStruct((p_rows, d), F32),
        compiler_params=_cparams(("arbitrary", "arbitrary")),
        name="moe_expert_swiglu",
    )(blk_e, blk_x, blk_v, xs, wg, wu, wd)


def _combine_kernel(dest_ref, ys_ref, x_ref, route_ref, mod_ref, fw_ref, o_ref, y1_ref, y2_ref, sem, *,
                    d, tb, final):
    def copy(t, kk, buf):
        return pltpu.make_async_copy(ys_ref.at[pl.ds(dest_ref[0, 2 * t + kk], 1), :],
                                     buf.at[pl.ds(t, 1), :], sem)

    def start(t, c):
        copy(t, 0, y1_ref).start(priority=0)
        copy(t, 1, y2_ref).start(priority=1)
        return c

    def wait(t, c):
        copy(t, 0, y1_ref).wait()
        copy(t, 1, y2_ref).wait()
        return c

    lax.fori_loop(0, tb, start, 0, unroll=DMA_UNROLL)
    lax.fori_loop(0, tb, wait, 0, unroll=DMA_UNROLL)
    w1 = route_ref[:, R_W1:R_W1 + 1]
    w2 = route_ref[:, R_W2:R_W2 + 1]
    y = x_ref[...] + mod_ref[:, 5 * d:6 * d] * (y1_ref[...] * w1 + y2_ref[...] * w2)
    if final:
        y = y * lax.rsqrt(jnp.mean(y * y, axis=-1, keepdims=True) + EPS) * fw_ref[...]
    o_ref[...] = y


def _combine(ys, dest, x2, route, mod, final_w, rows_per_mod, final):
    n, d = x2.shape
    tb = _pick(rows_per_mod, (256, 128))
    per = rows_per_mod // tb
    return pl.pallas_call(
        functools.partial(_combine_kernel, d=d, tb=tb, final=final),
        grid=(n // tb,),
        in_specs=[pl.BlockSpec((None, 1, 2 * tb), lambda i: (i, 0, 0), memory_space=pltpu.SMEM),
                  pl.BlockSpec(memory_space=pl.ANY),
                  pl.BlockSpec((tb, d), lambda i: (i, 0)),
                  pl.BlockSpec((tb, LANES), lambda i: (i, 0)),
                  pl.BlockSpec((None, 1, 6 * d), lambda i: (i // per, 0, 0)),
                  pl.BlockSpec((1, d), lambda i: (0, 0))],
        out_specs=pl.BlockSpec((tb, d), lambda i: (i, 0)),
        out_shape=jax.ShapeDtypeStruct((n, d), F32),
        scratch_shapes=[pltpu.VMEM((tb, d), F32), pltpu.VMEM((tb, d), F32), pltpu.SemaphoreType.DMA(())],
        compiler_params=_cparams(("arbitrary",)),
        name="moe_combine_rows",
    )(dest.reshape(n // tb, 1, 2 * tb), ys, x2, route, mod, final_w)


def _moe_ffn(x, mod, nw, final_w, router_w, router_b, wg, wu, wd, final):
    b, t, d = x.shape
    n = b * t
    x2 = x.reshape(n, d)
    h, route, counts = _route(x2, mod, nw, router_w, router_b, t)
    mb = _pick(n, (1024, 512, 256, 128))
    cnt = counts[0, :N_EXPERTS].astype(jnp.int32)
    nblk_e = (cnt + mb - 1) // mb
    blk_end = jnp.cumsum(nblk_e)
    pad_start = (blk_end - nblk_e) * mb
    nblk = (2 * n) // mb + N_EXPERTS
    bi = jnp.arange(nblk, dtype=jnp.int32)
    total_blk = blk_end[-1]
    blk_v = (bi < total_blk).astype(jnp.int32)
    clamped = jnp.minimum(bi, total_blk - 1).astype(jnp.int32)
    blk_e = jnp.sum((clamped[:, None] >= blk_end[None, :]).astype(jnp.int32), axis=1)
    blk_e = jnp.minimum(blk_e, N_EXPERTS - 1).astype(jnp.int32)
    e12 = route[:, R_E1:R_E2 + 1].astype(jnp.int32)
    rank12 = route[:, R_RANK1:R_RANK2 + 1].astype(jnp.int32)
    dest = (pad_start[e12] + rank12).reshape(n * 2)
    xs = _dispatch(h, dest, nblk * mb)
    ys = _experts(xs, blk_e, clamped, blk_v, wg, wu, wd, mb)
    out = _combine(ys, dest, x2, route, mod, final_w, t, final)
    return out.reshape(b, t, d)


def _pad_in_weight(w):
    d = w.shape[0]
    ab_end = 4 * DN_DIM + 4 * DN_HEADS
    return jnp.concatenate([w[:, :ab_end], jnp.zeros((d, SEG_AB[1] - ab_end), w.dtype), w[:, ab_end:]],
                           axis=1).astype(BF16)


def kernel(x, c, ctx, c_ctx, w_mod, b_mod, norm1_w, norm2_w, w_in, dn_conv_w, dn_a_log, dn_dt_bias, dn_norm_w, at_qnorm_w, at_knorm_w, w_out, ffn_w_gate, ffn_w_up, ffn_w_down, moe_router_w, moe_router_b, moe_w_gate, moe_w_up, moe_w_down, final_norm_w):
    b, t, d = x.shape
    depth = w_mod.shape[0]
    xc = ctx
    rows = -(-(b + 1) // SUBLANES) * SUBLANES
    c_rows = jnp.zeros((rows, d), F32).at[:b].set(c).at[b].set(c_ctx)
    mod_all = _modulation(c_rows, w_mod, b_mod)
    final_w = final_norm_w.reshape(1, d)
    zero_state = jnp.zeros((b, 2, DN_GROUPS, MXU_DIM, MXU_DIM), F32)

    for layer in range(depth):
        last = layer == depth - 1
        mod = mod_all[layer, :b].reshape(b, 1, 6 * d)
        mod_c = jnp.broadcast_to(mod_all[layer, b].reshape(1, 1, 6 * d), (b, 1, 6 * d))
        n1 = norm1_w[layer].reshape(1, d)
        n2 = norm2_w[layer].reshape(1, d)
        w_pad = _pad_in_weight(w_in[layer])
        w_o = w_out[layer].astype(BF16)

        qkv, z, ab, atq, atk, atv = _in_projection(x, mod, n1, w_pad)
        qkv_c, z_c, ab_c, atq_c, atk_c, atv_c = _in_projection(xc, mod_c, n1, w_pad)

        prep_c = _dn_prepare(qkv_c, ab_c, dn_conv_w[layer], dn_a_log[layer], dn_dt_bias[layer])
        of_c, ob_c, s_ctx = _dn_scan(prep_c, zero_state)
        prep = _dn_prepare(qkv, ab, dn_conv_w[layer], dn_a_log[layer], dn_dt_bias[layer])
        o_f, o_b, _ = _dn_scan(prep, s_ctx)

        q_l, kd_l, vd_l = _attn_prepare(atq, atk, atv, at_qnorm_w[layer], at_knorm_w[layer], rope=True)
        q_c, kd_c, vd_c = _attn_prepare(atq_c, atk_c, atv_c, at_qnorm_w[layer], at_knorm_w[layer], rope=False)
        at = _flash_attention(q_l, jnp.concatenate([kd_l, kd_c], axis=2), jnp.concatenate([vd_l, vd_c], axis=3))

        x = _out_projection(o_f, o_b, z, at, x, mod, dn_norm_w[layer], w_o)
        if not last:
            at_c = _flash_attention(q_c, kd_c, vd_c)
            xc = _out_projection(of_c, ob_c, z_c, at_c, xc, mod_c, dn_norm_w[layer], w_o)

        if layer % 2 == 0:
            i = layer // 2
            wg, wu, wd = (w[i].astype(BF16) for w in (ffn_w_gate, ffn_w_up, ffn_w_down))
            x = _dense_ffn(x, mod, n2, final_w, wg, wu, wd, last)
            if not last:
                xc = _dense_ffn(xc, mod_c, n2, final_w, wg, wu, wd, False)
        else:
            j = layer // 2
            wg, wu, wd = (w[j].astype(BF16) for w in (moe_w_gate, moe_w_up, moe_w_down))
            x = _moe_ffn(x, mod, n2, final_w, moe_router_w[j], moe_router_b[j], wg, wu, wd, last)
            if not last:
                xc = _moe_ffn(xc, mod_c, n2, final_w, moe_router_w[j], moe_router_b[j], wg, wu, wd, False)
    return x
```

```python
import functools
import math

import jax
import jax.numpy as jnp
from jax import lax
from jax.experimental import pallas as pl
from jax.experimental.pallas import tpu as pltpu

F32 = jnp.float32
BF16 = jnp.bfloat16
EPS = 1e-6
LOG2E = math.log2(math.e)

DN_HEADS = 8
HEAD_DIM = 64
DN_DIM = DN_HEADS * HEAD_DIM
DN_CONV_W = 5
DN_CHUNK = 64
AT_Q_HEADS = 8
AT_KV_HEADS = 2
AT_GROUP = AT_Q_HEADS // AT_KV_HEADS
AT_Q_DIM = AT_Q_HEADS * HEAD_DIM
AT_KV_DIM = AT_KV_HEADS * HEAD_DIM
GRID_W = 64
ROPE_THETA = 10000.0
N_EXPERTS = 8

LANES = 128
SUBLANES = 8
MXU_DIM = 256
VMEM_LIMIT = 52 * 1024 * 1024

HEADS_PER_GROUP = MXU_DIM // HEAD_DIM
DN_GROUPS = DN_HEADS // HEADS_PER_GROUP
FLASH_ROWS = 256
FLASH_BATCH = 8
PV_ROWS = HEAD_DIM + 16
DMA_UNROLL = 8

SEG_QKV = (0, 3 * DN_DIM)
SEG_Z = (3 * DN_DIM, 4 * DN_DIM)
SEG_AB = (4 * DN_DIM, 4 * DN_DIM + LANES)
SEG_ATQ = (SEG_AB[1], SEG_AB[1] + AT_Q_DIM)
SEG_ATK = (SEG_ATQ[1], SEG_ATQ[1] + AT_KV_DIM)
SEG_ATV = (SEG_ATK[1], SEG_ATK[1] + AT_KV_DIM)
IN_PAD = SEG_ATV[1]
SEGS = (SEG_QKV, SEG_Z, SEG_AB, SEG_ATQ, SEG_ATK, SEG_ATV)


def _cparams(sem, flags=None):
    return pltpu.CompilerParams(dimension_semantics=sem, vmem_limit_bytes=VMEM_LIMIT, flags=flags)


def _bdot(a, b):
    return jnp.dot(a.astype(BF16), b.astype(BF16), preferred_element_type=F32)


def _bdot_nt(a, b):
    return lax.dot_general(a.astype(BF16), b.astype(BF16), (((1,), (1,)), ((), ())),
                           preferred_element_type=F32)


def _bdot_tn(a, b):
    return lax.dot_general(a.astype(BF16), b.astype(BF16), (((0,), (0,)), ((), ())),
                           preferred_element_type=F32)


def _split3(x):
    hi = x.astype(BF16)
    r1 = x - hi.astype(F32)
    mid = r1.astype(BF16)
    lo = (r1 - mid.astype(F32)).astype(BF16)
    return hi, mid, lo


def _dot_sel(x, sel):
    dot = functools.partial(jnp.dot, preferred_element_type=F32)
    hi, mid, lo = _split3(x)
    return dot(hi, sel) + dot(mid, sel) + dot(lo, sel)


def _dot_sel_left(sel, x):
    dot = functools.partial(jnp.dot, preferred_element_type=F32)
    hi, mid, lo = _split3(x)
    return dot(sel, hi) + dot(sel, mid) + dot(sel, lo)


def _sigmoid(x):
    return 1.0 / (1.0 + jnp.exp(-x))


def _silu(x):
    return x * _sigmoid(x)


def _norm_modulate(x, nw, shift, scale):
    y = x * lax.rsqrt(jnp.mean(x * x, axis=-1, keepdims=True) + EPS)
    return (y * nw) * (1.0 + scale) + shift


def _pick(n, options):
    for o in options:
        if n % o == 0:
            return o
    raise ValueError(f"no tile in {options} divides {n}")


def _mod_kernel(c_ref, w_ref, b_ref, o_ref):
    o_ref[...] = _bdot(_silu(c_ref[...]), w_ref[...]) + b_ref[...]


def _modulation(c_rows, w_mod, b_mod):
    nl, d, n6 = w_mod.shape
    rows = c_rows.shape[0]
    tn = _pick(n6, (1536, 1024, 512, 256, 128))
    return pl.pallas_call(
        _mod_kernel,
        grid=(nl, n6 // tn),
        in_specs=[pl.BlockSpec((rows, d), lambda l, j: (0, 0)),
                  pl.BlockSpec((None, d, tn), lambda l, j: (l, 0, j)),
                  pl.BlockSpec((None, 1, tn), lambda l, j: (l, 0, j))],
        out_specs=pl.BlockSpec((None, rows, tn), lambda l, j: (l, 0, j)),
        out_shape=jax.ShapeDtypeStruct((nl, rows, n6), F32),
        compiler_params=_cparams(("parallel", "parallel")),
        name="adaln_modulation",
    )(c_rows, w_mod, b_mod.reshape(nl, 1, n6))


def _inproj_kernel(x_ref, mod_ref, nw_ref, w_ref, *out_refs, d):
    h = _norm_modulate(x_ref[...], nw_ref[...], mod_ref[:, 0:d], mod_ref[:, d:2 * d]).astype(BF16)
    for (a, b), o_ref in zip(SEGS, out_refs):
        o_ref[...] = jnp.dot(h, w_ref[:, a:b], preferred_element_type=F32)


def _in_projection(x, mod, nw, w_pad):
    b, t, d = x.shape
    tm = _pick(t, (512, 256, 128))
    return pl.pallas_call(
        functools.partial(_inproj_kernel, d=d),
        grid=(b, t // tm),
        in_specs=[pl.BlockSpec((None, tm, d), lambda i, j: (i, j, 0)),
                  pl.BlockSpec((None, 1, 6 * d), lambda i, j: (i, 0, 0)),
                  pl.BlockSpec((1, d), lambda i, j: (0, 0)),
                  pl.BlockSpec((d, IN_PAD), lambda i, j: (0, 0))],
        out_specs=[pl.BlockSpec((None, tm, s1 - s0), lambda i, j: (i, j, 0)) for s0, s1 in SEGS],
        out_shape=[jax.ShapeDtypeStruct((b, t, s1 - s0), F32) for s0, s1 in SEGS],
        compiler_params=_cparams(("parallel", "parallel")),
        name="norm_in_projection",
    )(x, mod, nw, w_pad)


def _dn_prep_kernel(xp_ref, x_ref, xn_ref, ab_ref, cw_ref, gp_ref, head_ones_ref, cum_f_ref, cum_b_ref,
                    expand_ref, q_ref, k_ref, v_ref, bf_ref, gf_ref, bb_ref, gb_ref, gc_ref, xe_ref, *, tm):
    i = pl.program_id(1)
    halo = SUBLANES
    pad = (DN_CONV_W - 1) // 2
    zeros = jnp.zeros((halo, 3 * DN_DIM), F32)
    xe_ref[0:halo, :] = jnp.where(i > 0, xp_ref[...], zeros)
    xe_ref[halo:halo + tm, :] = x_ref[...]
    xe_ref[halo + tm:halo + tm + halo, :] = jnp.where(i < pl.num_programs(1) - 1, xn_ref[...], zeros)
    acc = cw_ref[0:1, :] * xe_ref[halo - pad:halo - pad + tm, :]
    for j in range(1, DN_CONV_W):
        acc = acc + cw_ref[j:j + 1, :] * xe_ref[halo - pad + j:halo - pad + j + tm, :]
    qkv = _silu(acc)
    ones = head_ones_ref[...]
    q = qkv[:, 0:DN_DIM]
    k = qkv[:, DN_DIM:2 * DN_DIM]
    q_ref[...] = q * lax.rsqrt(_dot_sel(q * q, ones) + EPS) * (HEAD_DIM ** -0.5)
    k_ref[...] = k * lax.rsqrt(_dot_sel(k * k, ones) + EPS)
    v_ref[...] = qkv[:, 2 * DN_DIM:3 * DN_DIM]

    ab = ab_ref[...]
    lane = lax.broadcasted_iota(jnp.int32, ab.shape, 1)
    z = ab + gp_ref[1:2, :]
    softplus = jnp.maximum(z, 0.0) + jnp.log(1.0 + jnp.exp(-jnp.abs(z)))
    g = -jnp.exp(gp_ref[0:1, :]) * softplus
    g = jnp.where(lane < 2 * DN_HEADS, g, 0.0)
    gc = jnp.where(lane < DN_HEADS, _dot_sel_left(cum_f_ref[...], g), _dot_sel_left(cum_b_ref[...], g))
    gc_ref[...] = gc
    pieces = _split3(jnp.where(lane < 2 * DN_HEADS, gc, _sigmoid(ab)))
    for n, o_ref in enumerate((gf_ref, gb_ref, bf_ref, bb_ref)):
        e = expand_ref[n]
        o_ref[...] = sum(jnp.dot(p, e, preferred_element_type=F32) for p in pieces)


def _head_ones(width):
    idx = jnp.arange(width) // HEAD_DIM
    return (idx[:, None] == idx[None, :]).astype(BF16)


def _dn_prepare(qkv, ab, conv_w, a_log, dt_bias):
    b, t, c = qkv.shape
    tm = _pick(t, (256, 128, 64))
    nt = t // tm
    hb = tm // SUBLANES
    last8 = t // SUBLANES - 1
    gp = jnp.zeros((SUBLANES, LANES), F32)
    gp = gp.at[0, :2 * DN_HEADS].set(a_log.reshape(-1)).at[1, :2 * DN_HEADS].set(dt_bias.reshape(-1))
    r = jnp.arange(tm)
    same = (r[:, None] // DN_CHUNK) == (r[None, :] // DN_CHUNK)
    cum_f = (same & (r[None, :] <= r[:, None])).astype(BF16)
    cum_b = (same & (r[None, :] >= r[:, None])).astype(BF16)
    col_head = jnp.arange(DN_DIM) // HEAD_DIM
    src = jnp.arange(LANES)[:, None]
    expand = jnp.stack([(src == col_head[None, :] + off) for off in
                        (0, DN_HEADS, 2 * DN_HEADS, 3 * DN_HEADS)]).astype(BF16)
    wide = pl.BlockSpec((None, tm, DN_DIM), lambda i, j: (i, j, 0))
    const = lambda shape: pl.BlockSpec(shape, lambda i, j: (0,) * len(shape))
    outs = pl.pallas_call(
        functools.partial(_dn_prep_kernel, tm=tm),
        grid=(b, nt),
        in_specs=[pl.BlockSpec((None, SUBLANES, c), lambda i, j: (i, jnp.maximum(j * hb - 1, 0), 0)),
                  pl.BlockSpec((None, tm, c), lambda i, j: (i, j, 0)),
                  pl.BlockSpec((None, SUBLANES, c), lambda i, j: (i, jnp.minimum((j + 1) * hb, last8), 0)),
                  pl.BlockSpec((None, tm, LANES), lambda i, j: (i, j, 0)),
                  const((DN_CONV_W, c)), const((SUBLANES, LANES)), const((DN_DIM, DN_DIM)),
                  const((tm, tm)), const((tm, tm)), const((4, LANES, DN_DIM))],
        out_specs=[wide] * 7 + [pl.BlockSpec((None, tm, LANES), lambda i, j: (i, j, 0))],
        out_shape=[jax.ShapeDtypeStruct((b, t, DN_DIM), F32)] * 7 + [jax.ShapeDtypeStruct((b, t, LANES), F32)],
        scratch_shapes=[pltpu.VMEM((tm + 2 * SUBLANES, c), F32)],
        compiler_params=_cparams(("parallel", "parallel")),
        name="deltanet_prepare",
    )(qkv, qkv, qkv, ab, conv_w, gp, _head_ones(DN_DIM), cum_f, cum_b, expand)
    return outs


def _tile4(x):
    return jnp.concatenate([x] * HEADS_PER_GROUP, axis=0)


def _group_masks():
    n = HEADS_PER_GROUP * DN_CHUNK
    row = lax.broadcasted_iota(jnp.int32, (n, n), 0)
    col = lax.broadcasted_iota(jnp.int32, (n, n), 1)
    return row, col, (row // DN_CHUNK) == (col // DN_CHUNK)


def _collapse(full, same_head):
    kept = jnp.where(same_head, full, 0.0)
    out = kept[0:DN_CHUNK]
    for h in range(1, HEADS_PER_GROUP):
        out = out + kept[h * DN_CHUNK:(h + 1) * DN_CHUNK]
    return out


def _dn_local_kernel(q_ref, k_ref, v_ref, btf_ref, gxf_ref, btb_ref, gxb_ref, gr_ref,
                     uf_ref, wf_ref, af_ref, ub_ref, wb_ref, ab_ref, *, chunks):
    c = DN_CHUNK
    n = HEADS_PER_GROUP * c
    row, col, same_head = _group_masks()
    eye = (row == col).astype(F32)
    ci = lax.broadcasted_iota(jnp.int32, (c, n), 0)
    cj = lax.broadcasted_iota(jnp.int32, (c, n), 1) % c
    tri = {False: (ci >= cj, ci > cj), True: (ci <= cj, ci < cj)}

    def spread(x):
        return jnp.where(same_head, _tile4(x), 0.0)

    def joins(s):
        return ((row // (2 * s)) == (col // (2 * s))) & ((row // s) != (col // s))

    chains = []
    for cc in range(chunks):
        rs = slice(cc * c, (cc + 1) * c)
        for d, (bt_ref, gx_ref, outs) in enumerate(((btf_ref, gxf_ref, (uf_ref, wf_ref, af_ref)),
                                                    (btb_ref, gxb_ref, (ub_ref, wb_ref, ab_ref)))):
            for g in range(DN_GROUPS):
                chains.append((rs, slice(g * MXU_DIM, (g + 1) * MXU_DIM), d, g, cc, bt_ref, gx_ref, outs))

    a_mats, rhs_list = [], []
    for rs, sl, d, g, cc, bt_ref, gx_ref, outs in chains:
        incl, strict = tri[d == 1]
        q, k, v = q_ref[rs, sl], k_ref[rs, sl], v_ref[rs, sl]
        beta, gcx = bt_ref[rs, sl], gx_ref[rs, sl]
        diff = gcx - gr_ref[d, cc, :, sl]
        decay = jnp.where(incl, jnp.exp(jnp.where(incl, diff, 0.0)), 0.0)
        k_beta = k * beta
        prod = _bdot_nt(jnp.concatenate([k_beta, q], axis=0), spread(k)) * jnp.concatenate([decay, decay], axis=0)
        a_mats.append(spread(jnp.where(strict, prod[0:c], 0.0)))
        outs[2][rs, sl] = prod[c:2 * c].astype(BF16)
        rhs_list.append(jnp.concatenate([spread(v * beta), spread(k_beta * jnp.exp(gcx))], axis=1))

    def half_rows(x, s, second):
        o = s if second else 0
        return jnp.concatenate([x[r0 + o:r0 + o + s] for r0 in range(0, n, 2 * s)], axis=0)

    def merge_rows(x, new_half, s, second):
        parts = []
        for bi, r0 in enumerate(range(0, n, 2 * s)):
            new = new_half[bi * s:(bi + 1) * s]
            parts += [x[r0:r0 + s], new] if second else [new, x[r0 + s:r0 + 2 * s]]
        return jnp.concatenate(parts, axis=0)

    second = [d == 0 for _, _, d, _, _, _, _, _ in chains]
    invs = [eye - jnp.where(joins(1), a, 0.0) for a in a_mats]
    s = 2
    while s < c:
        js = joins(s)
        if s < SUBLANES:
            mids = [_bdot(jnp.where(js, a, 0.0), inv) for a, inv in zip(a_mats, invs)]
            invs = [inv - _bdot(inv, mid) for inv, mid in zip(invs, mids)]
        else:
            zeros = jnp.zeros((n, n), F32)
            mids = [merge_rows(zeros, _bdot(half_rows(jnp.where(js, a, 0.0), s, sec), inv), s, sec)
                    for a, inv, sec in zip(a_mats, invs, second)]
            invs = [merge_rows(inv, half_rows(inv, s, sec) - _bdot(half_rows(inv, s, sec), mid), s, sec)
                    for inv, mid, sec in zip(invs, mids, second)]
        s *= 2

    for (rs, sl, d, g, cc, bt_ref, gx_ref, outs), inv, rhs in zip(chains, invs, rhs_list):
        sol = _bdot(_collapse(inv, same_head), rhs)
        outs[0][rs, sl] = sol[:, 0:n]
        outs[1][rs, sl] = sol[:, n:2 * n].astype(BF16)


def _dn_local(prep):
    q, k, v, beta_f, gcx_f, beta_b, gcx_b, gc = prep
    b, t, _ = q.shape
    n = t // DN_CHUNK
    chunks = 4 if n % 4 == 0 else (2 if n % 2 == 0 else 1)
    rows = chunks * DN_CHUNK
    g_rows = gc[:, :, :2 * DN_HEADS].reshape(b, n, DN_CHUNK, 2, DN_HEADS)
    g_rows = g_rows.transpose(0, 3, 1, 4, 2).reshape(b, 2, n, 1, DN_DIM)
    wide = pl.BlockSpec((None, rows, DN_DIM), lambda i, j: (i, j, 0))
    return pl.pallas_call(
        functools.partial(_dn_local_kernel, chunks=chunks),
        grid=(b, n // chunks),
        in_specs=[wide] * 7 + [pl.BlockSpec((None, 2, chunks, 1, DN_DIM), lambda i, j: (i, 0, j, 0, 0))],
        out_specs=[wide] * 6,
        out_shape=[jax.ShapeDtypeStruct((b, t, DN_DIM), dt) for dt in (F32, BF16, BF16) * 2],
        compiler_params=_cparams(("parallel", "parallel")),
        name="deltanet_chunk_local",
    )(q, k, v, beta_f, gcx_f, beta_b, gcx_b, g_rows)


def _dn_scan_kernel(qf, kf, gxf, uf, wf, af, qb, kb, gxb, ub, wb, ab, s0_ref, of_ref, ob_ref, sfin_ref, s_ref,
                    *, chunks):
    step = pl.program_id(1)
    c = DN_CHUNK

    @pl.when(step == 0)
    def _():
        s_ref[...] = s0_ref[...]

    _, _, same_head = _group_masks()
    dirs = ((qf, kf, gxf, uf, wf, af, of_ref), (qb, kb, gxb, ub, wb, ab, ob_ref))
    states = {(d, g): s_ref[d, g] for d in range(2) for g in range(DN_GROUPS)}
    for cc in range(chunks):
        for d, (q, k, gx, u, w, at, o_ref) in enumerate(dirs):
            reverse = d == 1
            ck = chunks - 1 - cc if reverse else cc
            rs = slice(ck * c, (ck + 1) * c)
            for g in range(DN_GROUPS):
                sl = slice(g * MXU_DIM, (g + 1) * MXU_DIM)
                state = states[(d, g)]
                gcx = gx[rs, sl]
                g_last = gcx[0:1, :] if reverse else gcx[c - 1:c, :]
                q_dec = q[rs, sl] * jnp.exp(gcx)
                k_dec = k[rs, sl] * jnp.exp(g_last - gcx)
                ws = _bdot(jnp.concatenate([w[rs, sl], q_dec.astype(BF16)], axis=0), state)
                v_new = u[rs, sl] - ws[0:c]
                o_ref[rs, sl] = ws[c:2 * c] + _bdot(at[rs, sl], jnp.where(same_head, _tile4(v_new), 0.0))
                states[(d, g)] = state * jnp.exp(g_last) + jnp.where(same_head, _bdot_tn(k_dec, v_new), 0.0)
    for (d, g), state in states.items():
        s_ref[d, g] = state

    @pl.when(step == pl.num_programs(1) - 1)
    def _():
        sfin_ref[...] = s_ref[...]


def _dn_scan(prep, s0):
    q, k, _, _, gcx_f, _, gcx_b, _ = prep
    u_f, w_f, a_f, u_b, w_b, a_b = _dn_local(prep)
    b, t, _ = q.shape
    n = t // DN_CHUNK
    chunks = 4 if n % 4 == 0 else (2 if n % 2 == 0 else 1)
    rows = chunks * DN_CHUNK
    steps = n // chunks
    fwd = pl.BlockSpec((None, rows, DN_DIM), lambda i, j: (i, j, 0))
    bwd = pl.BlockSpec((None, rows, DN_DIM), lambda i, j: (i, steps - 1 - j, 0))
    state = pl.BlockSpec((None, 2, DN_GROUPS, MXU_DIM, MXU_DIM), lambda i, j: (i, 0, 0, 0, 0))
    o_f, o_b, s_fin = pl.pallas_call(
        functools.partial(_dn_scan_kernel, chunks=chunks),
        grid=(b, steps),
        in_specs=[fwd] * 6 + [bwd] * 6 + [state],
        out_specs=[fwd, bwd, state],
        out_shape=[jax.ShapeDtypeStruct((b, t, DN_DIM), F32)] * 2
        + [jax.ShapeDtypeStruct((b, 2, DN_GROUPS, MXU_DIM, MXU_DIM), F32)],
        scratch_shapes=[pltpu.VMEM((2, DN_GROUPS, MXU_DIM, MXU_DIM), F32)],
        compiler_params=_cparams(("parallel", "arbitrary")),
        name="deltanet_scan",
    )(q, k, gcx_f, u_f, w_f, a_f, q, k, gcx_b, u_b, w_b, a_b, s0)
    return o_f, o_b, s_fin


def _rope(x, cos_t, sin_t):
    width = x.shape[-1]
    half = HEAD_DIM // 2
    lane = lax.broadcasted_iota(jnp.int32, x.shape, 1)
    partner = jnp.where(lane % HEAD_DIM < half, pltpu.roll(x, width - half, 1), pltpu.roll(x, half, 1))
    reps = width // LANES
    cos_w = jnp.concatenate([cos_t] * reps, axis=1) if reps > 1 else cos_t
    sin_w = jnp.concatenate([sin_t] * reps, axis=1) if reps > 1 else sin_t
    return x * cos_w + partner * sin_w


def _head_rmsnorm(x, ones, w):
    ms = _dot_sel(x * x, ones) * (1.0 / HEAD_DIM)
    return x * lax.rsqrt(ms + EPS) * w


def _attn_prep_kernel(q_ref, k_ref, v_ref, qw_ref, kw_ref, cos_ref, sin_ref, ones_q_ref, ones_k_ref,
                      qo_ref, ko_ref, vo_ref, *, rope):
    q = _head_rmsnorm(q_ref[...], ones_q_ref[...], qw_ref[...])
    k = _head_rmsnorm(k_ref[...], ones_k_ref[...], kw_ref[...])
    if rope:
        q = _rope(q, cos_ref[...], sin_ref[...])
        k = _rope(k, cos_ref[...], sin_ref[...])
    qo_ref[...] = (q * (HEAD_DIM ** -0.5 * LOG2E)).astype(BF16)
    v = v_ref[...]
    lane = lax.broadcasted_iota(jnp.int32, k.shape, 1)
    first = lane < HEAD_DIM
    k_sw = pltpu.roll(k, HEAD_DIM, 1)
    v_sw = pltpu.roll(v, HEAD_DIM, 1)
    ko_ref[0] = jnp.where(first, k, k_sw).astype(BF16)
    ko_ref[1] = jnp.where(first, k_sw, k).astype(BF16)
    vo_ref[0] = jnp.where(first, v, 1.0).T[0:PV_ROWS, :].astype(BF16)
    vo_ref[1] = jnp.where(first, v_sw, 1.0).T[0:PV_ROWS, :].astype(BF16)


def _rope_tables(t):
    rows = t // GRID_W
    row_pos = jnp.repeat(jnp.arange(rows, dtype=F32), GRID_W, total_repeat_length=t)
    col_pos = jnp.tile(jnp.arange(GRID_W, dtype=F32), rows)
    n_freq = HEAD_DIM // 4
    freqs = ROPE_THETA ** (-jnp.arange(n_freq, dtype=F32) / n_freq)
    ang = jnp.concatenate([row_pos[:, None] * freqs, col_pos[:, None] * freqs], axis=-1)
    cos, sin = jnp.cos(ang), jnp.sin(ang)
    reps = LANES // HEAD_DIM
    return (jnp.tile(jnp.concatenate([cos, cos], axis=-1), (1, reps)),
            jnp.tile(jnp.concatenate([-sin, sin], axis=-1), (1, reps)))


def _attn_prepare(atq, atk, atv, qn_w, kn_w, rope):
    b, t, _ = atq.shape
    tm = _pick(t, (512, 256, 128))
    if rope:
        cos_t, sin_t = _rope_tables(t)
    else:
        cos_t = sin_t = jnp.zeros((t, LANES), F32)
    qw = jnp.tile(qn_w, AT_Q_HEADS).reshape(1, AT_Q_DIM)
    kw = jnp.tile(kn_w, AT_KV_HEADS).reshape(1, AT_KV_DIM)
    const = lambda shape: pl.BlockSpec(shape, lambda i, j: (0,) * len(shape))
    tab = pl.BlockSpec((tm, LANES), lambda i, j: (j, 0))
    kv_out = pl.BlockSpec((None, AT_KV_HEADS, tm, AT_KV_DIM), lambda i, j: (i, 0, j, 0))
    return pl.pallas_call(
        functools.partial(_attn_prep_kernel, rope=rope),
        grid=(b, t // tm),
        in_specs=[pl.BlockSpec((None, tm, AT_Q_DIM), lambda i, j: (i, j, 0)),
                  pl.BlockSpec((None, tm, AT_KV_DIM), lambda i, j: (i, j, 0)),
                  pl.BlockSpec((None, tm, AT_KV_DIM), lambda i, j: (i, j, 0)),
                  const((1, AT_Q_DIM)), const((1, AT_KV_DIM)), tab, tab,
                  const((AT_Q_DIM, AT_Q_DIM)), const((AT_KV_DIM, AT_KV_DIM))],
        out_specs=[pl.BlockSpec((None, tm, AT_Q_DIM), lambda i, j: (i, j, 0)), kv_out,
                   pl.BlockSpec((None, AT_KV_HEADS, PV_ROWS, tm), lambda i, j: (i, 0, 0, j))],
        out_shape=[jax.ShapeDtypeStruct((b, t, AT_Q_DIM), BF16),
                   jax.ShapeDtypeStruct((b, AT_KV_HEADS, t, AT_KV_DIM), BF16),
                   jax.ShapeDtypeStruct((b, AT_KV_HEADS, PV_ROWS, t), BF16)],
        compiler_params=_cparams(("parallel", "parallel")),
        name="attention_prepare",
    )(atq, atk, atv, qw, kw, cos_t, sin_t, _head_ones(AT_Q_DIM), _head_ones(AT_KV_DIM))


def _flash_kernel(q_ref, k_ref, vt_ref, o_ref, qs_ref, m_ref, acc_ref, *, tq):
    j = pl.program_id(3)

    @pl.when(j == 0)
    def _():
        q = q_ref[...]
        lane = lax.broadcasted_iota(jnp.int32, q.shape, 1)
        for h in range(AT_GROUP):
            qs_ref[h * tq:(h + 1) * tq, :] = jnp.where(lane // HEAD_DIM == h, q, jnp.zeros_like(q))
        m_ref[...] = jnp.full(m_ref.shape, -jnp.inf, F32)
        acc_ref[...] = jnp.zeros(acc_ref.shape, F32)

    k = k_ref[...]
    k2 = jnp.concatenate([k, k], axis=1)
    vt = vt_ref[...]
    all_blocks = [slice(r0, r0 + FLASH_ROWS) for r0 in range(0, AT_GROUP * tq, FLASH_ROWS)]
    for b0 in range(0, len(all_blocks), FLASH_BATCH):
        blocks = all_blocks[b0:b0 + FLASH_BATCH]
        scores = [lax.dot_general(k2, qs_ref[cols, :], (((1,), (1,)), ((), ())), preferred_element_type=F32)
                  for cols in blocks]
        m_prev = [m_ref[:, cols] for cols in blocks]
        acc_prev = [acc_ref[:, cols] for cols in blocks]
        m_next = [jnp.maximum(mp, jnp.max(s, axis=0, keepdims=True)) for mp, s in zip(m_prev, scores)]
        probs = [jnp.exp2(s - mn[0:1, :]).astype(BF16) for s, mn in zip(scores, m_next)]
        pv = [jnp.dot(vt, p, preferred_element_type=F32) for p in probs]
        for cols, mp, mn, ap, x in zip(blocks, m_prev, m_next, acc_prev, pv):
            acc_ref[:, cols] = jnp.exp2(mp[0:1, :] - mn[0:1, :]) * ap + x
            m_ref[:, cols] = mn

    @pl.when(j == pl.num_programs(3) - 1)
    def _():
        def normalised(h):
            a = acc_ref[:, h * tq:(h + 1) * tq]
            return a[0:HEAD_DIM, :] / a[HEAD_DIM:HEAD_DIM + 1, :]

        pairs = [jnp.concatenate([normalised(2 * p2), normalised(2 * p2 + 1)], axis=0).T
                 for p2 in range(AT_GROUP // 2)]
        o_ref[...] = jnp.concatenate(pairs, axis=1)


def _flash_attention(q, kd, vd):
    b, t, _ = q.shape
    s = kd.shape[2]
    tq = _pick(t, (1024, 512, 256, 128))
    tk = _pick(s, (1408, 768, 512, 256, 128))
    gw = AT_GROUP * HEAD_DIM
    return pl.pallas_call(
        functools.partial(_flash_kernel, tq=tq),
        grid=(b, AT_KV_HEADS, t // tq, s // tk),
        in_specs=[pl.BlockSpec((None, tq, gw), lambda i, g, a, j: (i, a, g)),
                  pl.BlockSpec((None, None, tk, AT_KV_DIM), lambda i, g, a, j: (i, g, j, 0)),
                  pl.BlockSpec((None, None, PV_ROWS, tk), lambda i, g, a, j: (i, g, 0, j))],
        out_specs=pl.BlockSpec((None, tq, gw), lambda i, g, a, j: (i, a, g)),
        out_shape=jax.ShapeDtypeStruct((b, t, AT_Q_DIM), F32),
        scratch_shapes=[pltpu.VMEM((AT_GROUP * tq, gw), BF16),
                        pltpu.VMEM((SUBLANES, AT_GROUP * tq), F32),
                        pltpu.VMEM((PV_ROWS, AT_GROUP * tq), F32)],
        compiler_params=_cparams(("parallel", "parallel", "parallel", "arbitrary")),
        name="gqa_flash_attention",
    )(q, kd, vd)


def _outproj_kernel(of_ref, ob_ref, z_ref, at_ref, x_ref, mod_ref, nw_ref, ones_ref, w_ref, o_ref, *, d):
    o = of_ref[...] + ob_ref[...]
    ms = _dot_sel(o * o, ones_ref[...]) * (1.0 / HEAD_DIM)
    dn = o * lax.rsqrt(ms + EPS) * nw_ref[...] * _silu(z_ref[...])
    mix = _bdot(dn, w_ref[0:DN_DIM, :]) + _bdot(at_ref[...], w_ref[DN_DIM:DN_DIM + AT_Q_DIM, :])
    o_ref[...] = x_ref[...] + mod_ref[:, 2 * d:3 * d] * mix


def _out_projection(o_f, o_b, z, at, x, mod, dn_norm_w, w_out):
    b, t, d = x.shape
    tm = _pick(t, (512, 256, 128))
    half = pl.BlockSpec((None, tm, DN_DIM), lambda i, j: (i, j, 0))
    full = pl.BlockSpec((None, tm, d), lambda i, j: (i, j, 0))
    const = lambda shape: pl.BlockSpec(shape, lambda i, j: (0,) * len(shape))
    return pl.pallas_call(
        functools.partial(_outproj_kernel, d=d),
        grid=(b, t // tm),
        in_specs=[half, half, half, half, full, pl.BlockSpec((None, 1, 6 * d), lambda i, j: (i, 0, 0)),
                  const((1, DN_DIM)), const((DN_DIM, DN_DIM)), const((DN_DIM + AT_Q_DIM, d))],
        out_specs=full,
        out_shape=jax.ShapeDtypeStruct((b, t, d), F32),
        compiler_params=_cparams(("parallel", "parallel")),
        name="mix_out_projection",
    )(o_f, o_b, z, at, x, mod, jnp.tile(dn_norm_w, DN_HEADS).reshape(1, DN_DIM), _head_ones(DN_DIM), w_out)


def _ffn_kernel(x_ref, mod_ref, nw_ref, fw_ref, wg_ref, wu_ref, wd_ref, o_ref, h_ref, acc_ref, *, d, final):
    f = pl.program_id(2)

    @pl.when(f == 0)
    def _():
        h_ref[...] = _norm_modulate(x_ref[...], nw_ref[...], mod_ref[:, 3 * d:4 * d],
                                    mod_ref[:, 4 * d:5 * d]).astype(BF16)
        acc_ref[...] = jnp.zeros(acc_ref.shape, F32)

    h = h_ref[...]
    gate = jnp.dot(h, wg_ref[...], preferred_element_type=F32)
    up = jnp.dot(h, wu_ref[...], preferred_element_type=F32)
    acc_ref[...] += jnp.dot((_silu(gate) * up).astype(BF16), wd_ref[...], preferred_element_type=F32)

    @pl.when(f == pl.num_programs(2) - 1)
    def _():
        y = x_ref[...] + mod_ref[:, 5 * d:6 * d] * acc_ref[...]
        if final:
            y = y * lax.rsqrt(jnp.mean(y * y, axis=-1, keepdims=True) + EPS) * fw_ref[...]
        o_ref[...] = y


def _dense_ffn(x, mod, nw, final_w, wg, wu, wd, final):
    b, t, d = x.shape
    fdim = wg.shape[1]
    tm = _pick(t, (1024, 512, 256, 128))
    tf = _pick(fdim, (512, 256, 128))
    full = pl.BlockSpec((None, tm, d), lambda i, j, f: (i, j, 0))
    const = lambda shape: pl.BlockSpec(shape, lambda i, j, f: (0,) * len(shape))
    return pl.pallas_call(
        functools.partial(_ffn_kernel, d=d, final=final),
        grid=(b, t // tm, fdim // tf),
        in_specs=[full, pl.BlockSpec((None, 1, 6 * d), lambda i, j, f: (i, 0, 0)), const((1, d)), const((1, d)),
                  pl.BlockSpec((d, tf), lambda i, j, f: (0, f)),
                  pl.BlockSpec((d, tf), lambda i, j, f: (0, f)),
                  pl.BlockSpec((tf, d), lambda i, j, f: (f, 0))],
        out_specs=full,
        out_shape=jax.ShapeDtypeStruct((b, t, d), F32),
        scratch_shapes=[pltpu.VMEM((tm, d), BF16), pltpu.VMEM((tm, d), F32)],
        compiler_params=_cparams(("parallel", "parallel", "arbitrary")),
        name="dense_swiglu",
    )(x, mod, nw, final_w, wg, wu, wd)


R_E1, R_E2, R_W1, R_W2, R_RANK1, R_RANK2 = range(6)


def _router_kernel(x_ref, mod_ref, nw_ref, rw_ref, rb_ref, tri_ref, h_ref, route_ref, cnt_ref, carry_ref, *, d):
    @pl.when(pl.program_id(0) == 0)
    def _():
        carry_ref[...] = jnp.zeros(carry_ref.shape, F32)

    h = _norm_modulate(x_ref[...], nw_ref[...], mod_ref[:, 3 * d:4 * d], mod_ref[:, 4 * d:5 * d])
    h_ref[...] = h
    logits = _bdot(h, rw_ref[...]) + rb_ref[...]
    lane = lax.broadcasted_iota(jnp.int32, logits.shape, 1).astype(F32)
    neg = jnp.float32(-jnp.inf)
    lg = jnp.where(lane < N_EXPERTS, logits, neg)
    m1 = jnp.max(lg, axis=1, keepdims=True)
    i1 = jnp.min(jnp.where(lg == m1, lane, float(LANES)), axis=1, keepdims=True)
    hot1 = lane == i1
    lg2 = jnp.where(hot1, neg, lg)
    m2 = jnp.max(lg2, axis=1, keepdims=True)
    i2 = jnp.min(jnp.where(lg2 == m2, lane, float(LANES)), axis=1, keepdims=True)
    hot2 = lane == i2
    e2 = jnp.exp(m2 - m1)
    w1 = 1.0 / (1.0 + e2)
    w2 = e2 / (1.0 + e2)
    member = jnp.where(hot1 | hot2, 1.0, 0.0)
    before = jnp.dot(tri_ref[...], member.astype(BF16), preferred_element_type=F32) + carry_ref[0:1, :]
    rank1 = jnp.sum(jnp.where(hot1, before, 0.0), axis=1, keepdims=True)
    rank2 = jnp.sum(jnp.where(hot2, before, 0.0), axis=1, keepdims=True)
    rec = jnp.zeros(logits.shape, F32)
    for ln, val in ((R_E1, i1), (R_E2, i2), (R_W1, w1), (R_W2, w2), (R_RANK1, rank1), (R_RANK2, rank2)):
        rec = jnp.where(lane == float(ln), val, rec)
    route_ref[...] = rec
    tb = member.shape[0]
    total = before[tb - 1:tb, :] + member[tb - 1:tb, :]
    carry_ref[...] = jnp.broadcast_to(total, carry_ref.shape)
    cnt_ref[...] = jnp.broadcast_to(total, cnt_ref.shape)


def _route(x2, mod, nw, router_w, router_b, rows_per_mod):
    n, d = x2.shape
    tb = _pick(rows_per_mod, (512, 256, 128))
    per = rows_per_mod // tb
    rw = jnp.zeros((d, LANES), BF16).at[:, :N_EXPERTS].set(router_w.astype(BF16))
    rb = jnp.zeros((1, LANES), F32).at[0, :N_EXPERTS].set(router_b)
    r = jnp.arange(tb)
    tri = (r[None, :] < r[:, None]).astype(BF16)
    const = lambda shape: pl.BlockSpec(shape, lambda i: (0,) * len(shape))
    return pl.pallas_call(
        functools.partial(_router_kernel, d=d),
        grid=(n // tb,),
        in_specs=[pl.BlockSpec((tb, d), lambda i: (i, 0)),
                  pl.BlockSpec((None, 1, 6 * d), lambda i: (i // per, 0, 0)),
                  const((1, d)), const((d, LANES)), const((1, LANES)), const((tb, tb))],
        out_specs=[pl.BlockSpec((tb, d), lambda i: (i, 0)), pl.BlockSpec((tb, LANES), lambda i: (i, 0)),
                   const((SUBLANES, LANES))],
        out_shape=[jax.ShapeDtypeStruct((n, d), F32), jax.ShapeDtypeStruct((n, LANES), F32),
                   jax.ShapeDtypeStruct((SUBLANES, LANES), F32)],
        scratch_shapes=[pltpu.VMEM((SUBLANES, LANES), F32)],
        compiler_params=_cparams(("arbitrary",)),
        name="moe_router_top2",
    )(x2, mod, nw, rw, rb, tri)


def _dispatch_kernel(dest_ref, h_ref, xs_in_ref, xs_ref, sem, *, tb):
    del xs_in_ref

    def copy(t, kk):
        return pltpu.make_async_copy(h_ref.at[pl.ds(t, 1), :],
                                     xs_ref.at[pl.ds(dest_ref[0, 2 * t + kk], 1), :], sem)

    def start(t, c):
        copy(t, 0).start(priority=0)
        copy(t, 1).start(priority=1)
        return c

    def wait(t, c):
        copy(t, 0).wait()
        copy(t, 1).wait()
        return c

    lax.fori_loop(0, tb, start, 0, unroll=DMA_UNROLL)
    lax.fori_loop(0, tb, wait, 0, unroll=DMA_UNROLL)


def _dispatch(h, dest, p_rows):
    n, d = h.shape
    tb = _pick(n, (256, 128))
    return pl.pallas_call(
        functools.partial(_dispatch_kernel, tb=tb),
        grid=(n // tb,),
        in_specs=[pl.BlockSpec((None, 1, 2 * tb), lambda i: (i, 0, 0), memory_space=pltpu.SMEM),
                  pl.BlockSpec((tb, d), lambda i: (i, 0)), pl.BlockSpec(memory_space=pl.ANY)],
        out_specs=pl.BlockSpec(memory_space=pl.ANY),
        out_shape=jax.ShapeDtypeStruct((p_rows, d), F32),
        scratch_shapes=[pltpu.SemaphoreType.DMA(())],
        input_output_aliases={2: 0},
        compiler_params=_cparams(("arbitrary",)),
        name="moe_dispatch_rows",
    )(dest.reshape(n // tb, 1, 2 * tb), h, jnp.zeros((p_rows, d), F32))


def _expert_kernel(be_ref, bx_ref, bv_ref, x_ref, wg_ref, wu_ref, wd_ref, o_ref, h_ref, acc_ref):
    i, f = pl.program_id(0), pl.program_id(1)
    valid = bv_ref[i] == 1
    last = f == pl.num_programs(1) - 1

    @pl.when(valid & (f == 0))
    def _():
        h_ref[...] = x_ref[...].astype(BF16)
        acc_ref[...] = jnp.zeros(acc_ref.shape, F32)

    @pl.when(valid)
    def _():
        h = h_ref[...]
        gate = jnp.dot(h, wg_ref[...], preferred_element_type=F32)
        up = jnp.dot(h, wu_ref[...], preferred_element_type=F32)
        acc_ref[...] += jnp.dot((_silu(gate) * up).astype(BF16), wd_ref[...], preferred_element_type=F32)

    @pl.when(valid & last)
    def _():
        o_ref[...] = acc_ref[...]

    @pl.when(jnp.logical_not(valid) & last)
    def _():
        o_ref[...] = jnp.zeros(o_ref.shape, F32)


def _experts(xs, blk_e, blk_x, blk_v, wg, wu, wd, mb):
    p_rows, d = xs.shape
    fdim = wg.shape[2]
    tf = _pick(fdim, (512, 256, 128))
    nf = fdim // tf
    fsel = lambda i, f, bv: jnp.where(bv[i] == 1, f, nf - 1)
    grid_spec = pltpu.PrefetchScalarGridSpec(
        num_scalar_prefetch=3,
        grid=(p_rows // mb, nf),
        in_specs=[pl.BlockSpec((mb, d), lambda i, f, be, bx, bv: (bx[i], 0)),
                  pl.BlockSpec((None, d, tf), lambda i, f, be, bx, bv: (be[i], 0, fsel(i, f, bv))),
                  pl.BlockSpec((None, d, tf), lambda i, f, be, bx, bv: (be[i], 0, fsel(i, f, bv))),
                  pl.BlockSpec((None, tf, d), lambda i, f, be, bx, bv: (be[i], fsel(i, f, bv), 0))],
        out_specs=pl.BlockSpec((mb, d), lambda i, f, be, bx, bv: (i, 0)),
        scratch_shapes=[pltpu.VMEM((mb, d), BF16), pltpu.VMEM((mb, d), F32)])
    return pl.pallas_call(
        _expert_kernel,
        grid_spec=grid_spec,
        out_shape=jax.ShapeDtypeStruct((p_rows, d), F32),
        compiler_params=_cparams(("arbitrary", "arbitrary")),
        name="moe_expert_swiglu",
    )(blk_e, blk_x, blk_v, xs, wg, wu, wd)


def _combine_kernel(dest_ref, ys_ref, x_ref, route_ref, mod_ref, fw_ref, o_ref, y1_ref, y2_ref, sem, *,
                    d, tb, final):
    def copy(t, kk, buf):
        return pltpu.make_async_copy(ys_ref.at[pl.ds(dest_ref[0, 2 * t + kk], 1), :],
                                     buf.at[pl.ds(t, 1), :], sem)

    def start(t, c):
        copy(t, 0, y1_ref).start(priority=0)
        copy(t, 1, y2_ref).start(priority=1)
        return c

    def wait(t, c):
        copy(t, 0, y1_ref).wait()
        copy(t, 1, y2_ref).wait()
        return c

    lax.fori_loop(0, tb, start, 0, unroll=DMA_UNROLL)
    lax.fori_loop(0, tb, wait, 0, unroll=DMA_UNROLL)
    w1 = route_ref[:, R_W1:R_W1 + 1]
    w2 = route_ref[:, R_W2:R_W2 + 1]
    y = x_ref[...] + mod_ref[:, 5 * d:6 * d] * (y1_ref[...] * w1 + y2_ref[...] * w2)
    if final:
        y = y * lax.rsqrt(jnp.mean(y * y, axis=-1, keepdims=True) + EPS) * fw_ref[...]
    o_ref[...] = y


def _combine(ys, dest, x2, route, mod, final_w, rows_per_mod, final):
    n, d = x2.shape
    tb = _pick(rows_per_mod, (256, 128))
    per = rows_per_mod // tb
    return pl.pallas_call(
        functools.partial(_combine_kernel, d=d, tb=tb, final=final),
        grid=(n // tb,),
        in_specs=[pl.BlockSpec((None, 1, 2 * tb), lambda i: (i, 0, 0), memory_space=pltpu.SMEM),
                  pl.BlockSpec(memory_space=pl.ANY),
                  pl.BlockSpec((tb, d), lambda i: (i, 0)),
                  pl.BlockSpec((tb, LANES), lambda i: (i, 0)),
                  pl.BlockSpec((None, 1, 6 * d), lambda i: (i // per, 0, 0)),
                  pl.BlockSpec((1, d), lambda i: (0, 0))],
        out_specs=pl.BlockSpec((tb, d), lambda i: (i, 0)),
        out_shape=jax.ShapeDtypeStruct((n, d), F32),
        scratch_shapes=[pltpu.VMEM((tb, d), F32), pltpu.VMEM((tb, d), F32), pltpu.SemaphoreType.DMA(())],
        compiler_params=_cparams(("arbitrary",)),
        name="moe_combine_rows",
    )(dest.reshape(n // tb, 1, 2 * tb), ys, x2, route, mod, final_w)


def _moe_ffn(x, mod, nw, final_w, router_w, router_b, wg, wu, wd, final):
    b, t, d = x.shape
    n = b * t
    x2 = x.reshape(n, d)
    h, route, counts = _route(x2, mod, nw, router_w, router_b, t)
    mb = _pick(n, (1024, 512, 256, 128))
    cnt = counts[0, :N_EXPERTS].astype(jnp.int32)
    nblk_e = (cnt + mb - 1) // mb
    blk_end = jnp.cumsum(nblk_e)
    pad_start = (blk_end - nblk_e) * mb
    nblk = (2 * n) // mb + N_EXPERTS
    bi = jnp.arange(nblk, dtype=jnp.int32)
    total_blk = blk_end[-1]
    blk_v = (bi < total_blk).astype(jnp.int32)
    clamped = jnp.minimum(bi, total_blk - 1).astype(jnp.int32)
    blk_e = jnp.sum((clamped[:, None] >= blk_end[None, :]).astype(jnp.int32), axis=1)
    blk_e = jnp.minimum(blk_e, N_EXPERTS - 1).astype(jnp.int32)
    e12 = route[:, R_E1:R_E2 + 1].astype(jnp.int32)
    rank12 = route[:, R_RANK1:R_RANK2 + 1].astype(jnp.int32)
    dest = (pad_start[e12] + rank12).reshape(n * 2)
    xs = _dispatch(h, dest, nblk * mb)
    ys = _experts(xs, blk_e, clamped, blk_v, wg, wu, wd, mb)
    out = _combine(ys, dest, x2, route, mod, final_w, t, final)
    return out.reshape(b, t, d)


def _pad_in_weight(w):
    d = w.shape[0]
    ab_end = 4 * DN_DIM + 4 * DN_HEADS
    return jnp.concatenate([w[:, :ab_end], jnp.zeros((d, SEG_AB[1] - ab_end), w.dtype), w[:, ab_end:]],
                           axis=1).astype(BF16)


def kernel(x, c, ctx, c_ctx, w_mod, b_mod, norm1_w, norm2_w, w_in, dn_conv_w, dn_a_log, dn_dt_bias, dn_norm_w, at_qnorm_w, at_knorm_w, w_out, ffn_w_gate, ffn_w_up, ffn_w_down, moe_router_w, moe_router_b, moe_w_gate, moe_w_up, moe_w_down, final_norm_w):
    b, t, d = x.shape
    depth = w_mod.shape[0]
    xc = ctx
    rows = -(-(b + 1) // SUBLANES) * SUBLANES
    c_rows = jnp.zeros((rows, d), F32).at[:b].set(c).at[b].set(c_ctx)
    mod_all = _modulation(c_rows, w_mod, b_mod)
    final_w = final_norm_w.reshape(1, d)
    zero_state = jnp.zeros((b, 2, DN_GROUPS, MXU_DIM, MXU_DIM), F32)

    for layer in range(depth):
        last = layer == depth - 1
        mod = mod_all[layer, :b].reshape(b, 1, 6 * d)
        mod_c = jnp.broadcast_to(mod_all[layer, b].reshape(1, 1, 6 * d), (b, 1, 6 * d))
        n1 = norm1_w[layer].reshape(1, d)
        n2 = norm2_w[layer].reshape(1, d)
        w_pad = _pad_in_weight(w_in[layer])
        w_o = w_out[layer].astype(BF16)

        qkv, z, ab, atq, atk, atv = _in_projection(x, mod, n1, w_pad)
        qkv_c, z_c, ab_c, atq_c, atk_c, atv_c = _in_projection(xc, mod_c, n1, w_pad)

        prep_c = _dn_prepare(qkv_c, ab_c, dn_conv_w[layer], dn_a_log[layer], dn_dt_bias[layer])
        of_c, ob_c, s_ctx = _dn_scan(prep_c, zero_state)
        prep = _dn_prepare(qkv, ab, dn_conv_w[layer], dn_a_log[layer], dn_dt_bias[layer])
        o_f, o_b, _ = _dn_scan(prep, s_ctx)

        q_l, kd_l, vd_l = _attn_prepare(atq, atk, atv, at_qnorm_w[layer], at_knorm_w[layer], rope=True)
        q_c, kd_c, vd_c = _attn_prepare(atq_c, atk_c, atv_c, at_qnorm_w[layer], at_knorm_w[layer], rope=False)
        at = _flash_attention(q_l, jnp.concatenate([kd_l, kd_c], axis=2), jnp.concatenate([vd_l, vd_c], axis=3))

        x = _out_projection(o_f, o_b, z, at, x, mod, dn_norm_w[layer], w_o)
        if not last:
            at_c = _flash_attention(q_c, kd_c, vd_c)
            xc = _out_projection(of_c, ob_c, z_c, at_c, xc, mod_c, dn_norm_w[layer], w_o)

        if layer % 2 == 0:
            i = layer // 2
            wg, wu, wd = (w[i].astype(BF16) for w in (ffn_w_gate, ffn_w_up, ffn_w_down))
            x = _dense_ffn(x, mod, n2, final_w, wg, wu, wd, last)
            if not last:
                xc = _dense_ffn(xc, mod_c, n2, final_w, wg, wu, wd, False)
        else:
            j = layer // 2
            wg, wu, wd = (w[j].astype(BF16) for w in (moe_w_gate, moe_w_up, moe_w_down))
            x = _moe_ffn(x, mod, n2, final_w, moe_router_w[j], moe_router_b[j], wg, wu, wd, last)
            if not last:
                xc = _moe_ffn(xc, mod_c, n2, final_w, moe_router_w[j], moe_router_b[j], wg, wu, wd, False)
    return x
```
